```python
import math
import jax, jax.numpy as jnp
from jax import lax
import numpy as np

D_MODEL = 1024
BATCH = 8
SEQ = 8192
DEPTH = 1
DEC_BATCH = 8
DEC_SEQ = 64
PAST_LEN = 2048

CHUNK = 64
QBLOCK = 128
HEAD_DIM = 64
SB_HEADS = 8
SB_WIDTH = SB_HEADS * HEAD_DIM
CONV_CH = 512
CONV_WIDTH = 31
MIX_WIDTH = CONV_CH + SB_WIDTH
IN_WIDTH = 2 * CONV_CH + 3 * SB_WIDTH
D_FF = 2816
PLE_DIM = 256
EPS = 1e-6
SB_SCALE = 1.0 / math.sqrt(HEAD_DIM)

kernel_name = "hybrid_conformer_conv_stickbreaking_step"


def rms_norm(x, g):
    xf = x.astype(jnp.float32)
    y = xf * lax.rsqrt(jnp.mean(xf * xf, axis=-1, keepdims=True) + EPS)
    return (y * g.astype(jnp.float32)).astype(x.dtype)


def layer_norm(x, g, b):
    xf = x.astype(jnp.float32)
    mu = jnp.mean(xf, axis=-1, keepdims=True)
    xc = xf - mu
    y = xc * lax.rsqrt(jnp.mean(xc * xc, axis=-1, keepdims=True) + EPS)
    return (y * g.astype(jnp.float32) + b.astype(jnp.float32)).astype(x.dtype)


def swiglu(x, w_gu, w_down):
    g, u = jnp.split(x @ w_gu, 2, axis=-1)
    return (jax.nn.silu(g) * u) @ w_down


def conv_module(a, hist, w_dw, b_dw, ln_g, ln_b):
    a1, a2 = jnp.split(a, 2, axis=-1)
    u = a1 * jax.nn.sigmoid(a2)
    upad = jnp.concatenate([hist.astype(u.dtype), u], axis=1)
    y = lax.conv_general_dilated(
        upad, w_dw[:, None, :].astype(u.dtype), window_strides=(1,), padding='VALID',
        dimension_numbers=('NWC', 'WIO', 'NWC'), feature_group_count=CONV_CH) + b_dw
    y = jax.nn.silu(layer_norm(y, ln_g, ln_b))
    new_hist = upad[:, -(CONV_WIDTH - 1):]
    return y, new_hist


def sb_attend(q, k, v, q_pos, k_pos):
    z = jnp.einsum('bqhd,bkhd->bhqk', q.astype(jnp.float32), k.astype(jnp.float32)) * SB_SCALE
    mask = k_pos[None, :] < q_pos[:, None]
    log_keep = jnp.where(mask, jax.nn.log_sigmoid(-z), 0.0)
    later = lax.cumsum(log_keep, axis=3, reverse=True) - log_keep
    w = jnp.where(mask, jnp.exp(jax.nn.log_sigmoid(z) + later), 0.0)
    return jnp.einsum('bhqk,bkhd->bqhd', w.astype(v.dtype), v)


def sb_blocked(q, k, v, q_pos, k_pos):
    b, t, h, d = q.shape
    nb = t // QBLOCK
    qb = q.reshape(b, nb, QBLOCK, h, d).transpose(1, 0, 2, 3, 4)
    pb = q_pos.reshape(nb, QBLOCK)
    out = lax.map(lambda xs: sb_attend(xs[0], k, v, xs[1], k_pos), (qb, pb))
    return out.transpose(1, 0, 2, 3, 4).reshape(b, t, h, d)


def trunk_layer(h, p, k_hist, v_hist, conv_hist, q_pos, k_pos, blocked,
                ffn1_norm, ffn1_w_gu, ffn1_w_down, mix_norm, w_in,
                conv_w, conv_b, conv_ln_g, conv_ln_b, w_out,
                ffn2_norm, ffn2_w_gu, ffn2_w_down, ple_norm, ple_gate_w, ple_w):
    b, l, _ = h.shape
    h = h + 0.5 * swiglu(rms_norm(h, ffn1_norm), ffn1_w_gu, ffn1_w_down)
    proj = rms_norm(h, mix_norm) @ w_in
    a_conv = proj[..., :2 * CONV_CH]
    qkv = proj[..., 2 * CONV_CH:].reshape(b, l, 3, SB_HEADS, HEAD_DIM)
    q, k, v = qkv[:, :, 0], qkv[:, :, 1], qkv[:, :, 2]
    conv_out, new_conv = conv_module(a_conv, conv_hist, conv_w, conv_b, conv_ln_g, conv_ln_b)
    if k_hist is None:
        k_all, v_all = k, v
    else:
        k_all = jnp.concatenate([k_hist.astype(k.dtype), k], axis=1)
        v_all = jnp.concatenate([v_hist.astype(v.dtype), v], axis=1)
    if blocked:
        sb = sb_blocked(q, k_all, v_all, q_pos, k_pos)
    else:
        sb = sb_attend(q, k_all, v_all, q_pos, k_pos)
    mixed = jnp.concatenate([conv_out, sb.reshape(b, l, SB_WIDTH)], axis=-1) @ w_out
    h = h + mixed
    h = h + 0.5 * swiglu(rms_norm(h, ffn2_norm), ffn2_w_gu, ffn2_w_down)
    gate = jax.nn.sigmoid(rms_norm(h, ple_norm) @ ple_gate_w)
    h = h + gate * (p @ ple_w)
    return h, k, v, new_conv


def setup_inputs(seed: int = 0) -> dict:
    key = jax.random.key(seed)
    ks = jax.random.split(key, 32)
    f32 = jnp.float32

    def nrm(k, shape, scale):
        return jax.random.normal(k, shape, f32) * scale

    def gain(k, shape):
        return 1.0 + 0.05 * jax.random.normal(k, shape, f32)

    return {
        "x_prompt": nrm(ks[0], (BATCH, SEQ, D_MODEL), 1.0),
        "x_sample": nrm(ks[1], (DEC_BATCH, DEC_SEQ, D_MODEL), 1.0),
        "p_prompt": nrm(ks[2], (DEPTH, BATCH, SEQ, PLE_DIM), 1.0),
        "p_sample": nrm(ks[3], (DEPTH, DEC_BATCH, DEC_SEQ, PLE_DIM), 1.0),
        "cache_k": nrm(ks[4], (DEPTH, DEC_BATCH, PAST_LEN, SB_HEADS, HEAD_DIM), 1.0),
        "cache_v": nrm(ks[5], (DEPTH, DEC_BATCH, PAST_LEN, SB_HEADS, HEAD_DIM), 1.0),
        "state_conv": nrm(ks[6], (DEPTH, DEC_BATCH, CONV_WIDTH - 1, CONV_CH), 0.5),
        "ffn1_norm": gain(ks[7], (DEPTH, D_MODEL)),
        "ffn1_w_gu": nrm(ks[8], (DEPTH, D_MODEL, 2 * D_FF), D_MODEL ** -0.5),
        "ffn1_w_down": nrm(ks[9], (DEPTH, D_FF, D_MODEL), D_FF ** -0.5),
        "mix_norm": gain(ks[10], (DEPTH, D_MODEL)),
        "w_in": nrm(ks[11], (DEPTH, D_MODEL, IN_WIDTH), D_MODEL ** -0.5),
        "conv_w": nrm(ks[12], (DEPTH, CONV_WIDTH, CONV_CH), CONV_WIDTH ** -0.5),
        "conv_b": nrm(ks[13], (DEPTH, CONV_CH), 0.02),
        "conv_ln_g": gain(ks[14], (DEPTH, CONV_CH)),
        "conv_ln_b": nrm(ks[15], (DEPTH, CONV_CH), 0.02),
        "w_out": nrm(ks[16], (DEPTH, MIX_WIDTH, D_MODEL), MIX_WIDTH ** -0.5),
        "ffn2_norm": gain(ks[17], (DEPTH, D_MODEL)),
        "ffn2_w_gu": nrm(ks[18], (DEPTH, D_MODEL, 2 * D_FF), D_MODEL ** -0.5),
        "ffn2_w_down": nrm(ks[19], (DEPTH, D_FF, D_MODEL), D_FF ** -0.5),
        "ple_norm": gain(ks[20], (DEPTH, D_MODEL)),
        "ple_gate_w": nrm(ks[21], (DEPTH, D_MODEL, D_MODEL), D_MODEL ** -0.5),
        "ple_w": nrm(ks[22], (DEPTH, PLE_DIM, D_MODEL), PLE_DIM ** -0.5),
        "final_norm": gain(ks[23], (D_MODEL,)),
    }


def reference(x_prompt, x_sample, p_prompt, p_sample, cache_k, cache_v, state_conv,
              ffn1_norm, ffn1_w_gu, ffn1_w_down, mix_norm, w_in,
              conv_w, conv_b, conv_ln_g, conv_ln_b, w_out,
              ffn2_norm, ffn2_w_gu, ffn2_w_down, ple_norm, ple_gate_w, ple_w, final_norm):
    b_p, t_p, _ = x_prompt.shape
    b_s, t_s, _ = x_sample.shape
    past = cache_k.shape[2]
    pos_prompt = jnp.arange(t_p, dtype=jnp.int32)
    q_pos_sample = past + jnp.arange(t_s, dtype=jnp.int32)
    k_pos_sample = jnp.arange(past + t_s, dtype=jnp.int32)

    h_p, h_s = x_prompt, x_sample
    kp_list, vp_list, cp_list, ks_list, vs_list, cs_list = [], [], [], [], [], []
    for i in range(DEPTH):
        lw = (ffn1_norm[i], ffn1_w_gu[i], ffn1_w_down[i], mix_norm[i], w_in[i],
              conv_w[i], conv_b[i], conv_ln_g[i], conv_ln_b[i], w_out[i],
              ffn2_norm[i], ffn2_w_gu[i], ffn2_w_down[i], ple_norm[i], ple_gate_w[i], ple_w[i])
        conv0 = jnp.zeros((b_p, CONV_WIDTH - 1, CONV_CH), x_prompt.dtype)
        h_p, k_p, v_p, c_p = trunk_layer(h_p, p_prompt[i], None, None, conv0,
                                         pos_prompt, pos_prompt, True, *lw)
        h_s, k_s, v_s, c_s = trunk_layer(h_s, p_sample[i], cache_k[i], cache_v[i], state_conv[i],
                                         q_pos_sample, k_pos_sample, False, *lw)
        kp_list.append(k_p); vp_list.append(v_p); cp_list.append(c_p)
        ks_list.append(k_s); vs_list.append(v_s); cs_list.append(c_s)

    y_prompt = rms_norm(h_p, final_norm)
    y_sample = rms_norm(h_s, final_norm)
    new_k_prompt = jnp.stack(kp_list)
    new_v_prompt = jnp.stack(vp_list)
    new_conv_prompt = jnp.stack(cp_list)
    new_k_sample = jnp.stack(ks_list)
    new_v_sample = jnp.stack(vs_list)
    new_conv_sample = jnp.stack(cs_list)
    return (y_prompt, y_sample, new_k_prompt, new_v_prompt, new_conv_prompt,
            new_k_sample, new_v_sample, new_conv_sample)
```

```python
import functools

import jax
import jax.numpy as jnp
from jax import lax
from jax.experimental import pallas as pl
from jax.experimental.pallas import tpu as pltpu

EPS = 1e-6
HEAD_DIM = 64
SB_HEADS = 8
SB_WIDTH = SB_HEADS * HEAD_DIM
CONV_CH = 512
CONV_WIDTH = 31
SB_SCALE = 0.125

LANES = 128
HEADS_PER_STEP = LANES // HEAD_DIM
HALO_ROWS = 32
VMEM_LIMIT_BYTES = 60 * 1024 * 1024

LOG_KEEP_FLOOR = -90.0

F32 = jnp.float32
BF16 = jnp.bfloat16


def _rms(x, g):
    return x * lax.rsqrt(jnp.mean(x * x, axis=-1, keepdims=True) + EPS) * g


def _dot(a, b):
    return jnp.dot(a, b, preferred_element_type=F32)


def _swiglu(n, wgu_ref, wd_ref, fc):
    d_ff = wd_ref.shape[0]
    acc = None
    for c in range(d_ff // fc):
        g = _dot(n, wgu_ref[:, c * fc:(c + 1) * fc])
        u = _dot(n, wgu_ref[:, d_ff + c * fc:d_ff + (c + 1) * fc])
        a = (g * jax.nn.sigmoid(g) * u).astype(BF16)
        d = _dot(a, wd_ref[c * fc:(c + 1) * fc, :])
        acc = d if acc is None else acc + d
    return acc


def _pre_kernel(x_ref, g1_ref, wgu_ref, wd_ref, gm_ref, win_ref,
                h1_ref, u_ref, k_ref, v_ref, qkv_ref, *, fc):
    x = x_ref[...]
    n1 = _rms(x, g1_ref[...]).astype(BF16)
    h1 = x + 0.5 * _swiglu(n1, wgu_ref, wd_ref, fc)
    h1_ref[...] = h1
    n2 = _rms(h1, gm_ref[...]).astype(BF16)
    c = CONV_CH
    a1 = _dot(n2, win_ref[:, 0:c])
    a2 = _dot(n2, win_ref[:, c:2 * c])
    u_ref[...] = a1 * jax.nn.sigmoid(a2)
    q = _dot(n2, win_ref[:, 2 * c:2 * c + SB_WIDTH])
    k = _dot(n2, win_ref[:, 2 * c + SB_WIDTH:2 * c + 2 * SB_WIDTH])
    v = _dot(n2, win_ref[:, 2 * c + 2 * SB_WIDTH:2 * c + 3 * SB_WIDTH])
    k_ref[...] = k
    v_ref[...] = v
    qkv_ref[:, 0:SB_WIDTH] = (q * SB_SCALE).astype(BF16)
    qkv_ref[:, SB_WIDTH:2 * SB_WIDTH] = k.astype(BF16)
    qkv_ref[:, 2 * SB_WIDTH:3 * SB_WIDTH] = v.astype(BF16)


def _resident(shape):
    zeros = (0,) * len(shape)
    return pl.BlockSpec(shape, lambda *_: zeros, pipeline_mode=pl.Buffered(1))


def _pre(x2d, g1, wgu, wd, gm, win, *, tm, fc):
    n, d = x2d.shape
    row = lambda w: pl.BlockSpec((tm, w), lambda i: (i, 0))
    return pl.pallas_call(
        functools.partial(_pre_kernel, fc=fc),
        grid=(n // tm,),
        in_specs=[row(d), _resident(g1.shape), _resident(wgu.shape), _resident(wd.shape),
                  _resident(gm.shape), _resident(win.shape)],
        out_specs=[row(d), row(CONV_CH), row(SB_WIDTH), row(SB_WIDTH), row(3 * SB_WIDTH)],
        out_shape=[jax.ShapeDtypeStruct((n, d), F32),
                   jax.ShapeDtypeStruct((n, CONV_CH), F32),
                   jax.ShapeDtypeStruct((n, SB_WIDTH), F32),
                   jax.ShapeDtypeStruct((n, SB_WIDTH), F32),
                   jax.ShapeDtypeStruct((n, 3 * SB_WIDTH), BF16)],
        compiler_params=pltpu.CompilerParams(
            dimension_semantics=("arbitrary",), vmem_limit_bytes=VMEM_LIMIT_BYTES),
        name="pre",
    )(x2d, g1, wgu, wd, gm, win)


def _attn_kernel(q_ref, k_ref, v_ref, o_ref, acc_ref, run_ref, *, tq, first_block):
    i = pl.program_id(2)
    lane = lax.broadcasted_iota(jnp.int32, (tq, LANES), 1)
    q2 = q_ref[...]
    zero = jnp.zeros_like(q2)
    qh = [jnp.where((lane // HEAD_DIM) == h, q2, zero) for h in range(HEADS_PER_STEP)]
    r = lax.broadcasted_iota(jnp.int32, (tq, tq), 0)
    c = lax.broadcasted_iota(jnp.int32, (tq, tq), 1)
    causal = c < r
    after = (r > c).astype(BF16)

    acc_ref[...] = jnp.zeros_like(acc_ref)
    run_ref[...] = jnp.zeros_like(run_ref)

    def block(j, diagonal):
        start = pl.multiple_of(j * tq, tq)
        kb = k_ref[pl.ds(start, tq), :]
        vb = v_ref[pl.ds(start, tq), :]
        for h in range(HEADS_PER_STEP):
            z = lax.dot_general(qh[h], kb, (((1,), (1,)), ((), ())), preferred_element_type=F32)
            lk = -(jnp.maximum(z, 0.0) + jnp.log1p(jnp.exp(-jnp.abs(z))))
            if diagonal:
                lk = jnp.where(causal, lk, 0.0)
            hi = lk.astype(BF16)
            lo = (lk - hi.astype(F32)).astype(BF16)
            run = run_ref[h]
            later = _dot(hi, after) + _dot(lo, after) + run
            w = jnp.exp(z + lk + later)
            if diagonal:
                w = jnp.where(causal, w, 0.0)
            acc_ref[h] += _dot(w.astype(BF16), vb)
            run_ref[h] = run + jnp.sum(lk, axis=1, keepdims=True)

    def slowest_decay():
        return jnp.max(run_ref[...])

    block(i + first_block, True)

    def cond(carry):
        j, m = carry
        return jnp.logical_and(j >= 0, m > LOG_KEEP_FLOOR)

    def body(carry):
        j, _ = carry
        block(j, False)
        return j - 1, slowest_decay()

    lax.while_loop(cond, body, (i + first_block - 1, slowest_decay()))

    out = acc_ref[0]
    for h in range(1, HEADS_PER_STEP):
        out = jnp.where((lane // HEAD_DIM) == h, acc_ref[h], out)
    o_ref[...] = out.astype(o_ref.dtype)


def _attention(q_arr, k_arr, v_arr, *, q_col, k_col, v_col, tq):
    b, t_q, _ = q_arr.shape
    t_k = k_arr.shape[1]
    assert t_q % tq == 0 and (t_k - t_q) % tq == 0
    n_pairs = SB_WIDTH // LANES
    kv_spec = lambda col: pl.BlockSpec((None, t_k, LANES), lambda bi, hp, i: (bi, 0, col + hp))
    return pl.pallas_call(
        functools.partial(_attn_kernel, tq=tq, first_block=(t_k - t_q) // tq),
        grid=(b, n_pairs, t_q // tq),
        in_specs=[pl.BlockSpec((None, tq, LANES), lambda bi, hp, i: (bi, i, q_col + hp)),
                  kv_spec(k_col), kv_spec(v_col)],
        out_specs=pl.BlockSpec((None, tq, LANES), lambda bi, hp, i: (bi, i, hp)),
        out_shape=jax.ShapeDtypeStruct((b, t_q, SB_WIDTH), BF16),
        scratch_shapes=[pltpu.VMEM((HEADS_PER_STEP, tq, LANES), F32),
                        pltpu.VMEM((HEADS_PER_STEP, tq, 1), F32)],
        compiler_params=pltpu.CompilerParams(
            dimension_semantics=("arbitrary", "arbitrary", "arbitrary"),
            vmem_limit_bytes=VMEM_LIMIT_BYTES),
        name="attn",
    )(q_arr, k_arr, v_arr)


def _post_kernel(h1_ref, u_ref, halo_ref, hist_ref, sb_ref, p_ref,
                 cw_ref, cb_ref, lg_ref, lb_ref, wo_ref,
                 g2_ref, wgu_ref, wd_ref, gp_ref, wpg_ref, wp_ref, gf_ref,
                 y_ref, buf_ref, *, tm, fc):
    j = pl.program_id(1)
    buf_ref[0:HALO_ROWS, :] = jnp.where(j == 0, hist_ref[...], halo_ref[...])
    buf_ref[HALO_ROWS:, :] = u_ref[...]
    pad = HALO_ROWS - (CONV_WIDTH - 1)
    y = cb_ref[...] + cw_ref[0:1, :] * buf_ref[pl.ds(pad, tm), :]
    for t in range(1, CONV_WIDTH):
        y = y + cw_ref[t:t + 1, :] * buf_ref[pl.ds(pad + t, tm), :]
    mu = jnp.mean(y, axis=-1, keepdims=True)
    yc = y - mu
    yn = yc * lax.rsqrt(jnp.mean(yc * yc, axis=-1, keepdims=True) + EPS) * lg_ref[...] + lb_ref[...]
    conv_out = (yn * jax.nn.sigmoid(yn)).astype(BF16)

    mixed = _dot(conv_out, wo_ref[0:CONV_CH, :]) + _dot(sb_ref[...], wo_ref[CONV_CH:, :])
    h2 = h1_ref[...] + mixed
    n2 = _rms(h2, g2_ref[...]).astype(BF16)
    h3 = h2 + 0.5 * _swiglu(n2, wgu_ref, wd_ref, fc)
    n3 = _rms(h3, gp_ref[...]).astype(BF16)
    gate = jax.nn.sigmoid(_dot(n3, wpg_ref[...]))
    h4 = h3 + gate * _dot(p_ref[...].astype(BF16), wp_ref[...])
    y_ref[...] = _rms(h4, gf_ref[...])


def _post(h1, u, hist, sb, p, cw, cb, lg, lb, wo, g2, wgu, wd, gp, wpg, wp, gf, *, tm, fc):
    b, t, d = h1.shape
    tile = lambda w: pl.BlockSpec((None, tm, w), lambda bi, j: (bi, j, 0))
    halo_blocks = tm // HALO_ROWS
    halo = pl.BlockSpec((None, HALO_ROWS, CONV_CH),
                        lambda bi, j: (bi, jnp.maximum(j * halo_blocks - 1, 0), 0))
    hist_spec = pl.BlockSpec((None, HALO_ROWS, CONV_CH), lambda bi, j: (bi, 0, 0))
    weights = (cw, cb, lg, lb, wo, g2, wgu, wd, gp, wpg, wp, gf)
    return pl.pallas_call(
        functools.partial(_post_kernel, tm=tm, fc=fc),
        grid=(b, t // tm),
        in_specs=[tile(d), tile(CONV_CH), halo, hist_spec, tile(SB_WIDTH), tile(p.shape[-1])]
                 + [_resident(w.shape) for w in weights],
        out_specs=tile(d),
        out_shape=jax.ShapeDtypeStruct((b, t, d), F32),
        scratch_shapes=[pltpu.VMEM((HALO_ROWS + tm, CONV_CH), F32)],
        compiler_params=pltpu.CompilerParams(
            dimension_semantics=("arbitrary", "arbitrary"), vmem_limit_bytes=VMEM_LIMIT_BYTES),
        name="post",
    )(h1, u, u, hist, sb, p, *weights)


def _layer(x, p, k_hist, v_hist, conv_hist, w, *, tm_pre, tm, tq, fc):
    b, t, d = x.shape
    h1, u, k, v, qkv = _pre(x.reshape(b * t, d), w["g1"], w["wgu1"], w["wd1"], w["gm"], w["win"],
                            tm=tm_pre, fc=fc)
    qkv = qkv.reshape(b, t, 3 * SB_WIDTH)
    n_pairs = SB_WIDTH // LANES
    if k_hist is None:
        sb = _attention(qkv, qkv, qkv, q_col=0, k_col=n_pairs, v_col=2 * n_pairs, tq=tq)
    else:
        past = k_hist.shape[1]
        k_all = jnp.concatenate([k_hist.reshape(b, past, SB_WIDTH).astype(BF16),
                                 qkv[:, :, SB_WIDTH:2 * SB_WIDTH]], axis=1)
        v_all = jnp.concatenate([v_hist.reshape(b, past, SB_WIDTH).astype(BF16),
                                 qkv[:, :, 2 * SB_WIDTH:]], axis=1)
        sb = _attention(qkv, k_all, v_all, q_col=0, k_col=0, v_col=0, tq=tq)
    u = u.reshape(b, t, CONV_CH)
    hist = jnp.zeros((b, HALO_ROWS, CONV_CH), F32)
    if conv_hist is not None:
        hist = hist.at[:, HALO_ROWS - (CONV_WIDTH - 1):, :].set(conv_hist)
    y = _post(h1.reshape(b, t, d), u, hist, sb, p,
              w["cw"], w["cb"], w["lg"], w["lb"], w["wo"], w["g2"], w["wgu2"], w["wd2"],
              w["gp"], w["wpg"], w["wp"], w["gf"], tm=tm, fc=fc)
    assert t >= CONV_WIDTH - 1
    new_conv = u[:, t - (CONV_WIDTH - 1):, :]
    shape5 = (1, b, t, SB_HEADS, HEAD_DIM)
    return y, k.reshape(shape5), v.reshape(shape5), new_conv[None]


def kernel(x_prompt, x_sample, p_prompt, p_sample, cache_k, cache_v, state_conv, ffn1_norm, ffn1_w_gu, ffn1_w_down, mix_norm, w_in, conv_w, conv_b, conv_ln_g, conv_ln_b, w_out, ffn2_norm, ffn2_w_gu, ffn2_w_down, ple_norm, ple_gate_w, ple_w, final_norm):
    assert ffn1_norm.shape[0] == 1, "single-layer stack"
    row = lambda a: a.reshape(1, -1)
    w = dict(
        g1=ffn1_norm, wgu1=ffn1_w_gu[0].astype(BF16), wd1=ffn1_w_down[0].astype(BF16),
        gm=mix_norm, win=w_in[0].astype(BF16),
        cw=conv_w[0], cb=conv_b, lg=conv_ln_g, lb=conv_ln_b, wo=w_out[0].astype(BF16),
        g2=ffn2_norm, wgu2=ffn2_w_gu[0].astype(BF16), wd2=ffn2_w_down[0].astype(BF16),
        gp=ple_norm, wpg=ple_gate_w[0].astype(BF16), wp=ple_w[0].astype(BF16), gf=row(final_norm))
    y_p, k_p, v_p, c_p = _layer(x_prompt, p_prompt[0], None, None, None, w,
                                tm_pre=512, tm=512, tq=128, fc=256)
    y_s, k_s, v_s, c_s = _layer(x_sample, p_sample[0], cache_k[0], cache_v[0], state_conv[0], w,
                                tm_pre=512, tm=64, tq=64, fc=256)
    return y_p, y_s, k_p, v_p, c_p, k_s, v_s, c_s
```

```python
import functools

import jax
import jax.numpy as jnp
from jax import lax
from jax.experimental import pallas as pl
from jax.experimental.pallas import tpu as pltpu

EPS = 1e-6
HEAD_DIM = 64
SB_HEADS = 8
SB_WIDTH = SB_HEADS * HEAD_DIM
CONV_CH = 512
CONV_WIDTH = 31
LOG2_E = 1.4426950408889634
Q_SCALE = LOG2_E / 8.0

LANES = 128
HEADS_PER_STEP = LANES // HEAD_DIM
TQ = LANES
HALO_ROWS = 32
SUBLANES = 8
CONV_ROWS = 32
SHIFT_ROWS = 64
VMEM_LIMIT_BYTES = 60 * 1024 * 1024

DECAY_DONE = 130.0
DEAD = 1e30

F32 = jnp.float32
BF16 = jnp.bfloat16


def _rms(x, g):
    return x * lax.rsqrt(jnp.mean(x * x, axis=-1, keepdims=True) + EPS) * g


def _dot(a, b):
    return jnp.dot(a, b, preferred_element_type=F32)


def _swiglu(n, wgu_ref, wd_ref, fc):
    d_ff = wd_ref.shape[0]
    acc = None
    for c in range(d_ff // fc):
        g = _dot(n, wgu_ref[:, c * fc:(c + 1) * fc])
        u = _dot(n, wgu_ref[:, d_ff + c * fc:d_ff + (c + 1) * fc])
        a = (g * jax.nn.sigmoid(g) * u).astype(BF16)
        d = _dot(a, wd_ref[c * fc:(c + 1) * fc, :])
        acc = d if acc is None else acc + d
    return acc


def _pre_kernel(x_ref, g1_ref, wgu_ref, wd_ref, gm_ref, win_ref,
                h1_ref, u_ref, k_ref, v_ref, qkv_ref, *, fc):
    x = x_ref[...]
    n1 = _rms(x, g1_ref[...]).astype(BF16)
    h1 = x + 0.5 * _swiglu(n1, wgu_ref, wd_ref, fc)
    h1_ref[...] = h1
    n2 = _rms(h1, gm_ref[...]).astype(BF16)
    c = CONV_CH
    a1 = _dot(n2, win_ref[:, 0:c])
    a2 = _dot(n2, win_ref[:, c:2 * c])
    u_ref[...] = a1 * jax.nn.sigmoid(a2)
    q = _dot(n2, win_ref[:, 2 * c:2 * c + SB_WIDTH])
    k = _dot(n2, win_ref[:, 2 * c + SB_WIDTH:2 * c + 2 * SB_WIDTH])
    v = _dot(n2, win_ref[:, 2 * c + 2 * SB_WIDTH:2 * c + 3 * SB_WIDTH])
    k_ref[...] = k
    v_ref[...] = v
    qkv_ref[:, 0:SB_WIDTH] = (q * Q_SCALE).astype(BF16)
    qkv_ref[:, SB_WIDTH:2 * SB_WIDTH] = k.astype(BF16)
    qkv_ref[:, 2 * SB_WIDTH:3 * SB_WIDTH] = v.astype(BF16)


def _resident(shape):
    zeros = (0,) * len(shape)
    return pl.BlockSpec(shape, lambda *_: zeros, pipeline_mode=pl.Buffered(1))


def _pre(x2d, g1, wgu, wd, gm, win, *, tm, fc):
    n, d = x2d.shape
    row = lambda w: pl.BlockSpec((tm, w), lambda i: (i, 0))
    return pl.pallas_call(
        functools.partial(_pre_kernel, fc=fc),
        grid=(n // tm,),
        in_specs=[row(d), _resident(g1.shape), _resident(wgu.shape), _resident(wd.shape),
                  _resident(gm.shape), _resident(win.shape)],
        out_specs=[row(d), row(CONV_CH), row(SB_WIDTH), row(SB_WIDTH), row(3 * SB_WIDTH)],
        out_shape=[jax.ShapeDtypeStruct((n, d), F32),
                   jax.ShapeDtypeStruct((n, CONV_CH), F32),
                   jax.ShapeDtypeStruct((n, SB_WIDTH), F32),
                   jax.ShapeDtypeStruct((n, SB_WIDTH), F32),
                   jax.ShapeDtypeStruct((n, 3 * SB_WIDTH), BF16)],
        compiler_params=pltpu.CompilerParams(
            dimension_semantics=("arbitrary",), vmem_limit_bytes=VMEM_LIMIT_BYTES),
        name="pre",
    )(x2d, g1, wgu, wd, gm, win)


def _after_and_ones(width):
    r = lax.broadcasted_iota(jnp.int32, (width, width + LANES), 0)
    c = lax.broadcasted_iota(jnp.int32, (width, width + LANES), 1)
    return jnp.logical_or(r > c, c >= width).astype(BF16)


def _attn_kernel(q_ref, k_ref, v_ref, o_ref, acc_ref, run_ref, *, n_sub, n_pair, first_block):
    i = pl.program_id(2)
    tq = TQ
    head_of_lane = lax.broadcasted_iota(jnp.int32, (tq, LANES), 1) // HEAD_DIM
    r = lax.broadcasted_iota(jnp.int32, (tq, tq), 0)
    c = lax.broadcasted_iota(jnp.int32, (tq, tq), 1)
    causal = c < r
    chains = [(s, pr, h) for s in range(n_sub) for pr in range(n_pair) for h in range(HEADS_PER_STEP)]

    acc_ref[...] = jnp.zeros_like(acc_ref)

    def q_of(s, pr, h):
        q2 = q_ref[s * tq:(s + 1) * tq, pr * LANES:(pr + 1) * LANES]
        return jnp.where(head_of_lane == h, q2, jnp.zeros_like(q2))

    def rows(ref, pr, blocks):
        cols = slice(pr * LANES, (pr + 1) * LANES)
        parts = [ref[pl.ds(pl.multiple_of(jb * tq, tq), tq), cols] for jb in blocks]
        return parts[0] if len(parts) == 1 else jnp.concatenate(parts, axis=0)

    def run_round(blocks_of, first_round):
        width = tq * len(blocks_of[0])
        kv = {(s, pr): (rows(k_ref, pr, blocks_of[s]), rows(v_ref, pr, blocks_of[s]))
              for s in range(n_sub) for pr in range(n_pair)}
        zb = [lax.dot_general(q_of(s, pr, h), kv[s, pr][0], (((1,), (1,)), ((), ())),
                              preferred_element_type=F32) for s, pr, h in chains]
        sp = [jnp.maximum(z, 0.0) + jnp.log2(1.0 + jnp.exp2(-jnp.abs(z))) for z in zb]
        if first_round:
            sp = [jnp.concatenate([x[:, :tq], jnp.where(causal, x[:, tq:], 0.0)], axis=1) for x in sp]
        hi = [x.astype(BF16) for x in sp]
        lo = [(x - y.astype(F32)).astype(BF16) for x, y in zip(sp, hi)]
        if first_round:
            after = _after_and_ones(width)[:, :width]
            later = [_dot(x, after) + _dot(y, after) for x, y in zip(hi, lo)]
            total = [jnp.broadcast_to(jnp.sum(x, axis=1, keepdims=True), (tq, LANES)) for x in sp]
        else:
            ao = _after_and_ones(width)
            ao2 = jnp.concatenate([ao, ao], axis=0)
            sums = [_dot(jnp.concatenate([x, y], axis=1), ao2) for x, y in zip(hi, lo)]
            later = [sm[:, :width] for sm in sums]
            total = [sm[:, width:] for sm in sums]
        e = [z - x - y for z, x, y in zip(zb, sp, later)]
        if first_round:
            dead_left = [jnp.where(first_block + i * n_sub + s == 0, DEAD, 0.0) for s, _, _ in chains]
            w = [jnp.concatenate([jnp.exp2(x[:, :tq] - d), jnp.where(causal, jnp.exp2(x[:, tq:]), 0.0)], axis=1)
                 for x, d in zip(e, dead_left)]
            run = total
        else:
            old = [run_ref[ch] for ch in chains]
            w = [jnp.exp2(x - o) for x, o in zip(e, old)]
            run = [o + t for o, t in zip(old, total)]
        for ch, x in zip(chains, w):
            acc_ref[ch] += _dot(x.astype(BF16), kv[ch[0], ch[1]][1])
        for ch, x in zip(chains, run):
            run_ref[ch] = jnp.where(blocks_of[ch[0]][0] <= 0, DEAD, x)

    diag = [first_block + i * n_sub + s for s in range(n_sub)]
    run_round([[jnp.maximum(jd - 1, 0), jd] for jd in diag], True)

    def least_decay():
        return jnp.min(run_ref[...])

    def body(carry):
        rnd, _ = carry
        run_round([[jnp.maximum(jd - 1 - rnd, 0)] for jd in diag], False)
        return rnd + 1, least_decay()

    lax.while_loop(lambda carry: carry[1] < DECAY_DONE, body, (jnp.int32(1), least_decay()))

    for s in range(n_sub):
        for pr in range(n_pair):
            out = jnp.where(head_of_lane == 0, acc_ref[s, pr, 0], acc_ref[s, pr, 1])
            o_ref[s * tq:(s + 1) * tq, pr * LANES:(pr + 1) * LANES] = out.astype(o_ref.dtype)


def _attention(q_arr, k_arr, v_arr, *, q_col, k_col, v_col, n_sub, n_pair):
    b, t_q, _ = q_arr.shape
    t_k = k_arr.shape[1]
    tile = n_sub * TQ
    assert t_q % tile == 0 and (t_k - t_q) % TQ == 0 and (SB_WIDTH // LANES) % n_pair == 0
    wcol = n_pair * LANES
    kv_spec = lambda col: pl.BlockSpec((None, t_k, wcol), lambda bi, hp, i: (bi, 0, col + hp))
    return pl.pallas_call(
        functools.partial(_attn_kernel, n_sub=n_sub, n_pair=n_pair, first_block=(t_k - t_q) // TQ),
        grid=(b, SB_WIDTH // wcol, t_q // tile),
        in_specs=[pl.BlockSpec((None, tile, wcol), lambda bi, hp, i: (bi, i, q_col + hp)),
                  kv_spec(k_col), kv_spec(v_col)],
        out_specs=pl.BlockSpec((None, tile, wcol), lambda bi, hp, i: (bi, i, hp)),
        out_shape=jax.ShapeDtypeStruct((b, t_q, SB_WIDTH), BF16),
        scratch_shapes=[pltpu.VMEM((n_sub, n_pair, HEADS_PER_STEP, TQ, LANES), F32),
                        pltpu.VMEM((n_sub, n_pair, HEADS_PER_STEP, TQ, LANES), F32)],
        compiler_params=pltpu.CompilerParams(
            dimension_semantics=("arbitrary", "arbitrary", "arbitrary"),
            vmem_limit_bytes=VMEM_LIMIT_BYTES),
        name="attn",
    )(q_arr, k_arr, v_arr)


def _conv_kernel(u_ref, halo_ref, hist_ref, cw_ref, cb_ref, lg_ref, lb_ref, o_ref, sh_ref):
    tm = u_ref.shape[0]
    pad = HALO_ROWS - (CONV_WIDTH - 1)
    sh_ref[0, 0:HALO_ROWS, :] = jnp.where(pl.program_id(1) == 0, hist_ref[...], halo_ref[...])
    sh_ref[0, HALO_ROWS:, :] = u_ref[...]
    for c0 in range(0, tm + HALO_ROWS - SUBLANES, SHIFT_ROWS):
        rows = min(SHIFT_ROWS, tm + HALO_ROWS - SUBLANES - c0)
        chunk = sh_ref[0, c0:c0 + rows + SUBLANES, :]
        for r in range(1, SUBLANES):
            sh_ref[r, c0:c0 + rows, :] = pltpu.roll(chunk, rows + SUBLANES - r, axis=0)[0:rows]
    for r0 in range(0, tm, CONV_ROWS):
        y = cb_ref[...]
        for t in range(CONV_WIDTH):
            r, a = (pad + t) % SUBLANES, (pad + t) // SUBLANES
            y = y + cw_ref[t:t + 1, :] * sh_ref[r, r0 + SUBLANES * a:r0 + SUBLANES * a + CONV_ROWS, :]
        yc = y - jnp.mean(y, axis=-1, keepdims=True)
        yn = yc * lax.rsqrt(jnp.mean(yc * yc, axis=-1, keepdims=True) + EPS) * lg_ref[...] + lb_ref[...]
        o_ref[r0:r0 + CONV_ROWS, :] = (yn * jax.nn.sigmoid(yn)).astype(BF16)


def _conv(u, hist, cw, cb, lg, lb, *, tm):
    b, t, c = u.shape
    assert t % tm == 0 and tm % CONV_ROWS == 0
    halo_blocks = tm // HALO_ROWS
    weights = (cw, cb, lg, lb)
    return pl.pallas_call(
        _conv_kernel,
        grid=(b, t // tm),
        in_specs=[pl.BlockSpec((None, tm, c), lambda bi, j: (bi, j, 0)),
                  pl.BlockSpec((None, HALO_ROWS, c), lambda bi, j: (bi, jnp.maximum(j * halo_blocks - 1, 0), 0)),
                  pl.BlockSpec((None, HALO_ROWS, c), lambda bi, j: (bi, 0, 0))]
                 + [_resident(w.shape) for w in weights],
        out_specs=pl.BlockSpec((None, tm, c), lambda bi, j: (bi, j, 0)),
        out_shape=jax.ShapeDtypeStruct((b, t, c), BF16),
        scratch_shapes=[pltpu.VMEM((SUBLANES, HALO_ROWS + tm, c), F32)],
        compiler_params=pltpu.CompilerParams(
            dimension_semantics=("arbitrary", "arbitrary"), vmem_limit_bytes=VMEM_LIMIT_BYTES),
        name="conv",
    )(u, u, hist, *weights)


def _post_kernel(h1_ref, conv_ref, sb_ref, p_ref, wo_ref,
                 g2_ref, wgu_ref, wd_ref, gp_ref, wpg_ref, wp_ref, gf_ref, y_ref, *, fc):
    mixed = _dot(conv_ref[...], wo_ref[0:CONV_CH, :]) + _dot(sb_ref[...], wo_ref[CONV_CH:, :])
    h2 = h1_ref[...] + mixed
    n2 = _rms(h2, g2_ref[...]).astype(BF16)
    h3 = h2 + 0.5 * _swiglu(n2, wgu_ref, wd_ref, fc)
    n3 = _rms(h3, gp_ref[...]).astype(BF16)
    gate = jax.nn.sigmoid(_dot(n3, wpg_ref[...]))
    h4 = h3 + gate * _dot(p_ref[...].astype(BF16), wp_ref[...])
    y_ref[...] = _rms(h4, gf_ref[...])


def _post(h1, conv_out, sb, p, wo, g2, wgu, wd, gp, wpg, wp, gf, *, tm, fc):
    n, d = h1.shape
    row = lambda w: pl.BlockSpec((tm, w), lambda i: (i, 0))
    weights = (wo, g2, wgu, wd, gp, wpg, wp, gf)
    return pl.pallas_call(
        functools.partial(_post_kernel, fc=fc),
        grid=(n // tm,),
        in_specs=[row(d), row(CONV_CH), row(SB_WIDTH), row(p.shape[-1])]
                 + [_resident(w.shape) for w in weights],
        out_specs=row(d),
        out_shape=jax.ShapeDtypeStruct((n, d), F32),
        compiler_params=pltpu.CompilerParams(
            dimension_semantics=("arbitrary",), vmem_limit_bytes=VMEM_LIMIT_BYTES),
        name="post",
    )(h1, conv_out, sb, p, *weights)


def _layer(x, p, k_hist, v_hist, conv_hist, w, *, tm, tm_conv, n_sub, n_pair, fc):
    b, t, d = x.shape
    h1, u, k, v, qkv = _pre(x.reshape(b * t, d), w["g1"], w["wgu1"], w["wd1"], w["gm"], w["win"],
                            tm=tm, fc=fc)
    qkv = qkv.reshape(b, t, 3 * SB_WIDTH)
    n_blocks = SB_WIDTH // LANES
    if k_hist is None:
        assert n_pair == 1
        sb = _attention(qkv, qkv, qkv, q_col=0, k_col=n_blocks, v_col=2 * n_blocks,
                        n_sub=n_sub, n_pair=n_pair)
    else:
        past = k_hist.shape[1]
        pad = (-t) % (n_sub * TQ)
        q, k_new, v_new = (jnp.pad(qkv[:, :, c * SB_WIDTH:(c + 1) * SB_WIDTH], ((0, 0), (0, pad), (0, 0)))
                           for c in range(3))
        k_all = jnp.concatenate([k_hist.reshape(b, past, SB_WIDTH).astype(BF16), k_new], axis=1)
        v_all = jnp.concatenate([v_hist.reshape(b, past, SB_WIDTH).astype(BF16), v_new], axis=1)
        sb = _attention(q, k_all, v_all, q_col=0, k_col=0, v_col=0, n_sub=n_sub, n_pair=n_pair)[:, :t]
    u = u.reshape(b, t, CONV_CH)
    hist = jnp.zeros((b, HALO_ROWS, CONV_CH), F32)
    if conv_hist is not None:
        hist = hist.at[:, HALO_ROWS - (CONV_WIDTH - 1):, :].set(conv_hist)
    conv_out = _conv(u, hist, w["cw"], w["cb"], w["lg"], w["lb"], tm=tm_conv)
    y = _post(h1, conv_out.reshape(b * t, CONV_CH), sb.reshape(b * t, SB_WIDTH), p.reshape(b * t, -1),
              w["wo"], w["g2"], w["wgu2"], w["wd2"], w["gp"], w["wpg"], w["wp"], w["gf"], tm=tm, fc=fc)
    assert t >= CONV_WIDTH - 1
    new_conv = u[:, t - (CONV_WIDTH - 1):, :]
    shape5 = (1, b, t, SB_HEADS, HEAD_DIM)
    return y.reshape(b, t, d), k.reshape(shape5), v.reshape(shape5), new_conv[None]


def kernel(x_prompt, x_sample, p_prompt, p_sample, cache_k, cache_v, state_conv, ffn1_norm, ffn1_w_gu, ffn1_w_down, mix_norm, w_in, conv_w, conv_b, conv_ln_g, conv_ln_b, w_out, ffn2_norm, ffn2_w_gu, ffn2_w_down, ple_norm, ple_gate_w, ple_w, final_norm):
    assert ffn1_norm.shape[0] == 1, "single-layer stack"
    row = lambda a: a.reshape(1, -1)
    w = dict(
        g1=ffn1_norm, wgu1=ffn1_w_gu[0].astype(BF16), wd1=ffn1_w_down[0].astype(BF16),
        gm=mix_norm, win=w_in[0].astype(BF16),
        cw=conv_w[0], cb=conv_b, lg=conv_ln_g, lb=conv_ln_b, wo=w_out[0].astype(BF16),
        g2=ffn2_norm, wgu2=ffn2_w_gu[0].astype(BF16), wd2=ffn2_w_down[0].astype(BF16),
        gp=ple_norm, wpg=ple_gate_w[0].astype(BF16), wp=ple_w[0].astype(BF16), gf=row(final_norm))
    y_p, k_p, v_p, c_p = _layer(x_prompt, p_prompt[0], None, None, None, w,
                                tm=512, tm_conv=512, n_sub=8, n_pair=1, fc=256)
    y_s, k_s, v_s, c_s = _layer(x_sample, p_sample[0], cache_k[0], cache_v[0], state_conv[0], w,
                                tm=512, tm_conv=64, n_sub=1, n_pair=4, fc=256)
    return y_p, y_s, k_p, v_p, c_p, k_s, v_s, c_s
```

```python
import functools

import jax
import jax.numpy as jnp
from jax import lax
from jax.experimental import pallas as pl
from jax.experimental.pallas import tpu as pltpu

EPS = 1e-6
HEAD_DIM = 64
SB_HEADS = 8
SB_WIDTH = SB_HEADS * HEAD_DIM
CONV_CH = 512
CONV_WIDTH = 31
LOG2_E = 1.4426950408889634
Q_SCALE = LOG2_E / 8.0

LANES = 128
HEADS_PER_STEP = LANES // HEAD_DIM
TQ = LANES
HALO_ROWS = 32
SUBLANES = 8
CONV_ROWS = 32
SHIFT_ROWS = 64
VMEM_LIMIT_BYTES = 60 * 1024 * 1024

DECAY_DONE = 130.0
DEAD = 1e30

F32 = jnp.float32
BF16 = jnp.bfloat16


def _rms(x, g):
    return x * lax.rsqrt(jnp.mean(x * x, axis=-1, keepdims=True) + EPS) * g


def _dot(a, b):
    return jnp.dot(a, b, preferred_element_type=F32)


def _swiglu(n, wgu_ref, wd_ref, fc):
    d_ff = wd_ref.shape[0]
    acc = None
    for c in range(d_ff // fc):
        g = _dot(n, wgu_ref[:, c * fc:(c + 1) * fc])
        u = _dot(n, wgu_ref[:, d_ff + c * fc:d_ff + (c + 1) * fc])
        a = (g * jax.nn.sigmoid(g) * u).astype(BF16)
        d = _dot(a, wd_ref[c * fc:(c + 1) * fc, :])
        acc = d if acc is None else acc + d
    return acc


def _pre_kernel(x_ref, g1_ref, wgu_ref, wd_ref, gm_ref, win_ref, wkvt_ref,
                h1_ref, u_ref, q_ref, v_ref, kt_ref, vt_ref, *maybe_ktb_ref, fc):
    x = x_ref[...]
    n1 = _rms(x, g1_ref[...]).astype(BF16)
    h1 = x + 0.5 * _swiglu(n1, wgu_ref, wd_ref, fc)
    h1_ref[...] = h1
    n2 = _rms(h1, gm_ref[...]).astype(BF16)
    c = CONV_CH
    a1 = _dot(n2, win_ref[:, 0:c])
    a2 = _dot(n2, win_ref[:, c:2 * c])
    u_ref[...] = a1 * jax.nn.sigmoid(a2)
    q_ref[...] = (_dot(n2, win_ref[:, 2 * c:2 * c + SB_WIDTH]) * Q_SCALE).astype(BF16)
    v_ref[...] = _dot(n2, win_ref[:, 2 * c + 2 * SB_WIDTH:2 * c + 3 * SB_WIDTH]).astype(BF16)
    nt = (((1,), (1,)), ((), ()))
    kt = lax.dot_general(wkvt_ref[0:SB_WIDTH, :], n2, nt, preferred_element_type=F32)
    kt_ref[...] = kt
    vt_ref[...] = lax.dot_general(wkvt_ref[SB_WIDTH:, :], n2, nt, preferred_element_type=F32)
    for ktb_ref in maybe_ktb_ref:
        for jj in range(ktb_ref.shape[0]):
            ktb_ref[jj] = kt[:, jj * TQ:(jj + 1) * TQ].astype(BF16)


def _resident(shape):
    zeros = (0,) * len(shape)
    return pl.BlockSpec(shape, lambda *_: zeros, pipeline_mode=pl.Buffered(1))


def _pre(x, g1, wgu, wd, gm, win, wkvt, *, tm, fc):
    b, t, d = x.shape
    tile = lambda w: pl.BlockSpec((None, tm, w), lambda bi, j: (bi, j, 0))
    tmajor = pl.BlockSpec((None, SB_WIDTH, tm), lambda bi, j: (bi, 0, j))
    out_specs = [tile(d), tile(CONV_CH), tile(SB_WIDTH), tile(SB_WIDTH), tmajor, tmajor]
    out_shape = [jax.ShapeDtypeStruct((b, t, d), F32), jax.ShapeDtypeStruct((b, t, CONV_CH), F32),
                 jax.ShapeDtypeStruct((b, t, SB_WIDTH), BF16), jax.ShapeDtypeStruct((b, t, SB_WIDTH), BF16),
                 jax.ShapeDtypeStruct((b, SB_WIDTH, t), F32), jax.ShapeDtypeStruct((b, SB_WIDTH, t), F32)]
    if tm % TQ == 0:
        out_specs.append(pl.BlockSpec((None, tm // TQ, SB_WIDTH, TQ), lambda bi, j: (bi, j, 0, 0)))
        out_shape.append(jax.ShapeDtypeStruct((b, t // TQ, SB_WIDTH, TQ), BF16))
    weights = (g1, wgu, wd, gm, win, wkvt)
    return pl.pallas_call(
        functools.partial(_pre_kernel, fc=fc),
        grid=(b, t // tm),
        in_specs=[tile(d)] + [_resident(w.shape) for w in weights],
        out_specs=out_specs,
        out_shape=out_shape,
        compiler_params=pltpu.CompilerParams(
            dimension_semantics=("arbitrary", "arbitrary"), vmem_limit_bytes=VMEM_LIMIT_BYTES),
        name="pre",
    )(x, *weights)


def _after_and_ones(width):
    r = lax.broadcasted_iota(jnp.int32, (width, width + LANES), 0)
    c = lax.broadcasted_iota(jnp.int32, (width, width + LANES), 1)
    return jnp.logical_or(r > c, c >= width).astype(BF16)


def _minus_abs(x):
    bits = lax.bitcast_convert_type(x, jnp.uint32) | jnp.uint32(0x80000000)
    return lax.bitcast_convert_type(bits, F32)


def _attn_kernel(q_ref, kt_ref, v_ref, o_ref, acc_ref, run_ref, *, n_sub, n_pair, first_block):
    i = pl.program_id(2)
    tq = TQ
    head_of_lane = lax.broadcasted_iota(jnp.int32, (tq, LANES), 1) // HEAD_DIM
    r = lax.broadcasted_iota(jnp.int32, (tq, tq), 0)
    c = lax.broadcasted_iota(jnp.int32, (tq, tq), 1)
    causal = c < r
    chains = [(s, pr, h) for s in range(n_sub) for pr in range(n_pair) for h in range(HEADS_PER_STEP)]

    acc_ref[...] = jnp.zeros_like(acc_ref)

    def q_of(s, pr, h):
        q2 = q_ref[s * tq:(s + 1) * tq, pr * LANES:(pr + 1) * LANES]
        return jnp.where(head_of_lane == h, q2, jnp.zeros_like(q2))

    def keys_t(pr, blocks):
        parts = [kt_ref[jb, pr * LANES:(pr + 1) * LANES, :] for jb in blocks]
        return parts[0] if len(parts) == 1 else jnp.concatenate(parts, axis=1)

    def values(pr, blocks):
        cols = slice(pr * LANES, (pr + 1) * LANES)
        parts = [v_ref[pl.ds(pl.multiple_of(jb * tq, tq), tq), cols] for jb in blocks]
        return parts[0] if len(parts) == 1 else jnp.concatenate(parts, axis=0)

    def run_round(blocks_of, first_round):
        width = tq * len(blocks_of[0])
        kv = {(s, pr): (keys_t(pr, blocks_of[s]), values(pr, blocks_of[s]))
              for s in range(n_sub) for pr in range(n_pair)}
        zb = [_dot(q_of(s, pr, h), kv[s, pr][0]) for s, pr, h in chains]
        sp = [jnp.maximum(z, 0.0) + jnp.log2(1.0 + jnp.exp2(_minus_abs(z))) for z in zb]
        if first_round:
            sp = [jnp.concatenate([x[:, :tq], jnp.where(causal, x[:, tq:], 0.0)], axis=1) for x in sp]
        hi = [x.astype(BF16) for x in sp]
        lo = [(x - y.astype(F32)).astype(BF16) for x, y in zip(sp, hi)]
        if first_round:
            after = _after_and_ones(width)[:, :width]
            later = [_dot(x, after) + _dot(y, after) for x, y in zip(hi, lo)]
            total = [jnp.broadcast_to(jnp.sum(x, axis=1, keepdims=True), (tq, LANES)) for x in sp]
        else:
            ao = _after_and_ones(width)
            ao2 = jnp.concatenate([ao, ao], axis=0)
            sums = [_dot(jnp.concatenate([x, y], axis=1), ao2) for x, y in zip(hi, lo)]
            later = [sm[:, :width] for sm in sums]
            total = [sm[:, width:] for sm in sums]
        e = [z - x - y for z, x, y in zip(zb, sp, later)]
        if first_round:
            dead_left = [jnp.where(first_block + i * n_sub + s == 0, DEAD, 0.0) for s, _, _ in chains]
            w = [jnp.concatenate([jnp.exp2(x[:, :tq] - d), jnp.where(causal, jnp.exp2(x[:, tq:]), 0.0)], axis=1)
                 for x, d in zip(e, dead_left)]
            run = total
        else:
            old = [run_ref[ch] for ch in chains]
            w = [jnp.exp2(x - o) for x, o in zip(e, old)]
            run = [o + t for o, t in zip(old, total)]
        for ch, x in zip(chains, w):
            acc_ref[ch] += _dot(x.astype(BF16), kv[ch[0], ch[1]][1])
        for ch, x in zip(chains, run):
            run_ref[ch] = jnp.where(blocks_of[ch[0]][0] <= 0, DEAD, x)

    diag = [first_block + i * n_sub + s for s in range(n_sub)]
    run_round([[jnp.maximum(jd - 1, 0), jd] for jd in diag], True)

    def least_decay():
        return jnp.min(run_ref[...])

    def body(carry):
        rnd, _ = carry
        run_round([[jnp.maximum(jd - 1 - rnd, 0)] for jd in diag], False)
        return rnd + 1, least_decay()

    lax.while_loop(lambda carry: carry[1] < DECAY_DONE, body, (jnp.int32(1), least_decay()))

    for s in range(n_sub):
        for pr in range(n_pair):
            out = jnp.where(head_of_lane == 0, acc_ref[s, pr, 0], acc_ref[s, pr, 1])
            o_ref[s * tq:(s + 1) * tq, pr * LANES:(pr + 1) * LANES] = out.astype(o_ref.dtype)


def _attention(q, kt, v, *, n_sub, n_pair):
    b, t_q, _ = q.shape
    t_k = v.shape[1]
    tile = n_sub * TQ
    assert t_q % tile == 0 and (t_k - t_q) % TQ == 0 and (SB_WIDTH // LANES) % n_pair == 0
    assert kt.shape == (b, t_k // TQ, SB_WIDTH, TQ)
    wcol = n_pair * LANES
    return pl.pallas_call(
        functools.partial(_attn_kernel, n_sub=n_sub, n_pair=n_pair, first_block=(t_k - t_q) // TQ),
        grid=(b, SB_WIDTH // wcol, t_q // tile),
        in_specs=[pl.BlockSpec((None, tile, wcol), lambda bi, hp, i: (bi, i, hp)),
                  pl.BlockSpec((None, t_k // TQ, wcol, TQ), lambda bi, hp, i: (bi, 0, hp, 0)),
                  pl.BlockSpec((None, t_k, wcol), lambda bi, hp, i: (bi, 0, hp))],
        out_specs=pl.BlockSpec((None, tile, wcol), lambda bi, hp, i: (bi, i, hp)),
        out_shape=jax.ShapeDtypeStruct((b, t_q, SB_WIDTH), BF16),
        scratch_shapes=[pltpu.VMEM((n_sub, n_pair, HEADS_PER_STEP, TQ, LANES), F32),
                        pltpu.VMEM((n_sub, n_pair, HEADS_PER_STEP, TQ, LANES), F32)],
        compiler_params=pltpu.CompilerParams(
            dimension_semantics=("arbitrary", "arbitrary", "arbitrary"),
            vmem_limit_bytes=VMEM_LIMIT_BYTES),
        name="attn",
    )(q, kt, v)


def _conv_kernel(u_ref, halo_ref, hist_ref, cw_ref, cb_ref, lg_ref, lb_ref, o_ref, sh_ref):
    tm = u_ref.shape[0]
    pad = HALO_ROWS - (CONV_WIDTH - 1)
    sh_ref[0, 0:HALO_ROWS, :] = jnp.where(pl.program_id(1) == 0, hist_ref[...], halo_ref[...])
    sh_ref[0, HALO_ROWS:, :] = u_ref[...]
    for c0 in range(0, tm + HALO_ROWS - SUBLANES, SHIFT_ROWS):
        rows = min(SHIFT_ROWS, tm + HALO_ROWS - SUBLANES - c0)
        chunk = sh_ref[0, c0:c0 + rows + SUBLANES, :]
        for r in range(1, SUBLANES):
            sh_ref[r, c0:c0 + rows, :] = pltpu.roll(chunk, rows + SUBLANES - r, axis=0)[0:rows]
    for r0 in range(0, tm, CONV_ROWS):
        y = cb_ref[...]
        for t in range(CONV_WIDTH):
            r, a = (pad + t) % SUBLANES, (pad + t) // SUBLANES
            y = y + cw_ref[t:t + 1, :] * sh_ref[r, r0 + SUBLANES * a:r0 + SUBLANES * a + CONV_ROWS, :]
        yc = y - jnp.mean(y, axis=-1, keepdims=True)
        yn = yc * lax.rsqrt(jnp.mean(yc * yc, axis=-1, keepdims=True) + EPS) * lg_ref[...] + lb_ref[...]
        o_ref[r0:r0 + CONV_ROWS, :] = (yn * jax.nn.sigmoid(yn)).astype(BF16)


def _conv(u, hist, cw, cb, lg, lb, *, tm):
    b, t, c = u.shape
    assert t % tm == 0 and tm % CONV_ROWS == 0
    halo_blocks = tm // HALO_ROWS
    weights = (cw, cb, lg, lb)
    return pl.pallas_call(
        _conv_kernel,
        grid=(b, t // tm),
        in_specs=[pl.BlockSpec((None, tm, c), lambda bi, j: (bi, j, 0)),
                  pl.BlockSpec((None, HALO_ROWS, c), lambda bi, j: (bi, jnp.maximum(j * halo_blocks - 1, 0), 0)),
                  pl.BlockSpec((None, HALO_ROWS, c), lambda bi, j: (bi, 0, 0))]
                 + [_resident(w.shape) for w in weights],
        out_specs=pl.BlockSpec((None, tm, c), lambda bi, j: (bi, j, 0)),
        out_shape=jax.ShapeDtypeStruct((b, t, c), BF16),
        scratch_shapes=[pltpu.VMEM((SUBLANES, HALO_ROWS + tm, c), F32)],
        compiler_params=pltpu.CompilerParams(
            dimension_semantics=("arbitrary", "arbitrary"), vmem_limit_bytes=VMEM_LIMIT_BYTES),
        name="conv",
    )(u, u, hist, *weights)


def _post_kernel(h1_ref, conv_ref, sb_ref, p_ref, wo_ref,
                 g2_ref, wgu_ref, wd_ref, gp_ref, wpg_ref, wp_ref, gf_ref, y_ref, *, fc):
    mixed = _dot(conv_ref[...], wo_ref[0:CONV_CH, :]) + _dot(sb_ref[...], wo_ref[CONV_CH:, :])
    h2 = h1_ref[...] + mixed
    n2 = _rms(h2, g2_ref[...]).astype(BF16)
    h3 = h2 + 0.5 * _swiglu(n2, wgu_ref, wd_ref, fc)
    n3 = _rms(h3, gp_ref[...]).astype(BF16)
    gate = jax.nn.sigmoid(_dot(n3, wpg_ref[...]))
    h4 = h3 + gate * _dot(p_ref[...].astype(BF16), wp_ref[...])
    y_ref[...] = _rms(h4, gf_ref[...])


def _post(h1, conv_out, sb, p, wo, g2, wgu, wd, gp, wpg, wp, gf, *, tm, fc):
    n, d = h1.shape
    row = lambda w: pl.BlockSpec((tm, w), lambda i: (i, 0))
    weights = (wo, g2, wgu, wd, gp, wpg, wp, gf)
    return pl.pallas_call(
        functools.partial(_post_kernel, fc=fc),
        grid=(n // tm,),
        in_specs=[row(d), row(CONV_CH), row(SB_WIDTH), row(p.shape[-1])]
                 + [_resident(w.shape) for w in weights],
        out_specs=row(d),
        out_shape=jax.ShapeDtypeStruct((n, d), F32),
        compiler_params=pltpu.CompilerParams(
            dimension_semantics=("arbitrary",), vmem_limit_bytes=VMEM_LIMIT_BYTES),
        name="post",
    )(h1, conv_out, sb, p, *weights)


def _layer(x, p, k_hist, v_hist, conv_hist, w, *, tm, tm_conv, n_sub, n_pair, fc):
    b, t, d = x.shape
    h1, u, q, v, kt, vt, *ktb = _pre(x, w["g1"], w["wgu1"], w["wd1"], w["gm"], w["win"], w["wkvt"], tm=tm, fc=fc)
    if k_hist is None:
        sb = _attention(q, ktb[0], v, n_sub=n_sub, n_pair=n_pair)
    else:
        past = k_hist.shape[1]
        pad = (-t) % (n_sub * TQ)
        kt_all = jnp.concatenate([k_hist.transpose(0, 2, 3, 1).reshape(b, SB_WIDTH, past).astype(BF16),
                                  kt.astype(BF16), jnp.zeros((b, SB_WIDTH, pad), BF16)], axis=2)
        kt_all = kt_all.reshape(b, SB_WIDTH, -1, TQ).transpose(0, 2, 1, 3)
        v_all = jnp.concatenate([v_hist.reshape(b, past, SB_WIDTH).astype(BF16), v,
                                 jnp.zeros((b, pad, SB_WIDTH), BF16)], axis=1)
        q = jnp.pad(q, ((0, 0), (0, pad), (0, 0)))
        sb = _attention(q, kt_all, v_all, n_sub=n_sub, n_pair=n_pair)[:, :t]
    hist = jnp.zeros((b, HALO_ROWS, CONV_CH), F32)
    if conv_hist is not None:
        hist = hist.at[:, HALO_ROWS - (CONV_WIDTH - 1):, :].set(conv_hist)
    conv_out = _conv(u, hist, w["cw"], w["cb"], w["lg"], w["lb"], tm=tm_conv)
    flat = lambda a: a.reshape(b * t, a.shape[-1])
    y = _post(flat(h1), flat(conv_out), flat(sb), flat(p),
              w["wo"], w["g2"], w["wgu2"], w["wd2"], w["gp"], w["wpg"], w["wp"], w["gf"],
              tm=min(512, b * t), fc=fc)
    assert t >= CONV_WIDTH - 1
    new_conv = u[:, t - (CONV_WIDTH - 1):, :]
    heads = lambda a: a.reshape(b, SB_HEADS, HEAD_DIM, t).transpose(0, 3, 1, 2)[None]
    return y.reshape(b, t, d), heads(kt), heads(vt), new_conv[None]


def kernel(x_prompt, x_sample, p_prompt, p_sample, cache_k, cache_v, state_conv, ffn1_norm, ffn1_w_gu, ffn1_w_down, mix_norm, w_in, conv_w, conv_b, conv_ln_g, conv_ln_b, w_out, ffn2_norm, ffn2_w_gu, ffn2_w_down, ple_norm, ple_gate_w, ple_w, final_norm):
    assert ffn1_norm.shape[0] == 1, "single-layer stack"
    row = lambda a: a.reshape(1, -1)
    w = dict(
        g1=ffn1_norm, wgu1=ffn1_w_gu[0].astype(BF16), wd1=ffn1_w_down[0].astype(BF16),
        gm=mix_norm, win=w_in[0].astype(BF16), wkvt=w_in[0][:, 2 * CONV_CH + SB_WIDTH:].T.astype(BF16),
        cw=conv_w[0], cb=conv_b, lg=conv_ln_g, lb=conv_ln_b, wo=w_out[0].astype(BF16),
        g2=ffn2_norm, wgu2=ffn2_w_gu[0].astype(BF16), wd2=ffn2_w_down[0].astype(BF16),
        gp=ple_norm, wpg=ple_gate_w[0].astype(BF16), wp=ple_w[0].astype(BF16), gf=row(final_norm))
    y_p, k_p, v_p, c_p = _layer(x_prompt, p_prompt[0], None, None, None, w,
                                tm=512, tm_conv=512, n_sub=8, n_pair=1, fc=256)
    y_s, k_s, v_s, c_s = _layer(x_sample, p_sample[0], cache_k[0], cache_v[0], state_conv[0], w,
                                tm=64, tm_conv=64, n_sub=1, n_pair=4, fc=256)
    return y_p, y_s, k_p, v_p, c_p, k_s, v_s, c_s
```

```python
import functools

import jax
import jax.numpy as jnp
from jax import lax
from jax.experimental import pallas as pl
from jax.experimental.pallas import tpu as pltpu

EPS = 1e-6
HEAD_DIM = 64
SB_HEADS = 8
SB_WIDTH = SB_HEADS * HEAD_DIM
CONV_CH = 512
CONV_WIDTH = 31
LOG2_E = 1.4426950408889634
Q_SCALE = LOG2_E / 8.0

LANES = 128
HEADS_PER_STEP = LANES // HEAD_DIM
TQ = LANES
HALO_ROWS = 32
SUBLANES = 8
CONV_ROWS = 32
SHIFT_ROWS = 64
VMEM_LIMIT_BYTES = 60 * 1024 * 1024

DECAY_DONE = 130.0
DEAD = 1e30

F32 = jnp.float32
BF16 = jnp.bfloat16
NT_DIMS = (((1,), (1,)), ((), ()))


def _rms(x, g):
    return x * lax.rsqrt(jnp.mean(x * x, axis=-1, keepdims=True) + EPS) * g


def _dot(a, b):
    return jnp.dot(a, b, preferred_element_type=F32)


def _swiglu(n, wgu_ref, wd_ref, fc):
    d_ff = wd_ref.shape[0]
    acc = None
    for c in range(d_ff // fc):
        g = _dot(n, wgu_ref[:, c * fc:(c + 1) * fc])
        u = _dot(n, wgu_ref[:, d_ff + c * fc:d_ff + (c + 1) * fc])
        a = (g * jax.nn.sigmoid(g) * u).astype(BF16)
        d = _dot(a, wd_ref[c * fc:(c + 1) * fc, :])
        acc = d if acc is None else acc + d
    return acc


def _pre_kernel(x_ref, g1_ref, wgu_ref, wd_ref, gm_ref, win_ref, wkvt_ref,
                h1_ref, u_ref, q_ref, kt_ref, vt_ref, ktb_ref, vtb_ref, *, fc):
    x = x_ref[...]
    n1 = _rms(x, g1_ref[...]).astype(BF16)
    h1 = x + 0.5 * _swiglu(n1, wgu_ref, wd_ref, fc)
    h1_ref[...] = h1
    n2 = _rms(h1, gm_ref[...]).astype(BF16)
    c = CONV_CH
    a1 = _dot(n2, win_ref[:, 0:c])
    a2 = _dot(n2, win_ref[:, c:2 * c])
    u_ref[...] = a1 * jax.nn.sigmoid(a2)
    q_ref[...] = (_dot(n2, win_ref[:, 2 * c:2 * c + SB_WIDTH]) * Q_SCALE).astype(BF16)
    kt = lax.dot_general(wkvt_ref[0:SB_WIDTH, :], n2, NT_DIMS, preferred_element_type=F32)
    vt = lax.dot_general(wkvt_ref[SB_WIDTH:, :], n2, NT_DIMS, preferred_element_type=F32)
    kt_ref[...] = kt
    vt_ref[...] = vt
    for jj in range(ktb_ref.shape[0]):
        ktb_ref[jj] = kt[:, jj * TQ:(jj + 1) * TQ].astype(BF16)
        vtb_ref[jj] = vt[:, jj * TQ:(jj + 1) * TQ].astype(BF16)


def _resident(shape):
    zeros = (0,) * len(shape)
    return pl.BlockSpec(shape, lambda *_: zeros, pipeline_mode=pl.Buffered(1))


def _pre(x, *, g1, wgu, wd, gm, win, wkvt, tm, fc):
    b, t, d = x.shape
    assert tm % TQ == 0 and t % tm == 0
    tile = lambda w: pl.BlockSpec((None, tm, w), lambda bi, j: (bi, j, 0))
    tmajor = pl.BlockSpec((None, SB_WIDTH, tm), lambda bi, j: (bi, 0, j))
    slabs = pl.BlockSpec((None, tm // TQ, SB_WIDTH, TQ), lambda bi, j: (bi, j, 0, 0))
    out_specs = [tile(d), tile(CONV_CH), tile(SB_WIDTH), tmajor, tmajor, slabs, slabs]
    out_shape = [jax.ShapeDtypeStruct((b, t, d), F32), jax.ShapeDtypeStruct((b, t, CONV_CH), F32),
                 jax.ShapeDtypeStruct((b, t, SB_WIDTH), BF16),
                 jax.ShapeDtypeStruct((b, SB_WIDTH, t), F32), jax.ShapeDtypeStruct((b, SB_WIDTH, t), F32),
                 jax.ShapeDtypeStruct((b, t // TQ, SB_WIDTH, TQ), BF16),
                 jax.ShapeDtypeStruct((b, t // TQ, SB_WIDTH, TQ), BF16)]
    weights = (g1, wgu, wd, gm, win, wkvt)
    return pl.pallas_call(
        functools.partial(_pre_kernel, fc=fc),
        grid=(b, t // tm),
        in_specs=[tile(d)] + [_resident(w.shape) for w in weights],
        out_specs=out_specs,
        out_shape=out_shape,
        compiler_params=pltpu.CompilerParams(
            dimension_semantics=("arbitrary", "arbitrary"), vmem_limit_bytes=VMEM_LIMIT_BYTES),
        name="pre",
    )(x, *weights)


def _after_and_ones(width):
    r = lax.broadcasted_iota(jnp.int32, (width, width + LANES), 0)
    c = lax.broadcasted_iota(jnp.int32, (width, width + LANES), 1)
    return jnp.logical_or(r > c, c >= width).astype(BF16)


def _minus_abs(x):
    bits = lax.bitcast_convert_type(x, jnp.uint32) | jnp.uint32(0x80000000)
    return lax.bitcast_convert_type(bits, F32)


def _attn_kernel(q_ref, kt_ref, vt_ref, o_ref, acc_ref, run_ref, *, n_sub, n_pair, first_block):
    i = pl.program_id(2)
    tq = TQ
    head_of_lane = lax.broadcasted_iota(jnp.int32, (tq, LANES), 1) // HEAD_DIM
    r = lax.broadcasted_iota(jnp.int32, (tq, tq), 0)
    c = lax.broadcasted_iota(jnp.int32, (tq, tq), 1)
    causal = c < r
    chains = [(s, pr, h) for s in range(n_sub) for pr in range(n_pair) for h in range(HEADS_PER_STEP)]

    acc_ref[...] = jnp.zeros_like(acc_ref)

    def q_of(s, pr, h):
        q2 = q_ref[s * tq:(s + 1) * tq, pr * LANES:(pr + 1) * LANES]
        return jnp.where(head_of_lane == h, q2, jnp.zeros_like(q2))

    def slabs(ref, pr, blocks):
        parts = [ref[jb, pr * LANES:(pr + 1) * LANES, :] for jb in blocks]
        return parts[0] if len(parts) == 1 else jnp.concatenate(parts, axis=1)

    def run_round(blocks_of, first_round):
        width = tq * len(blocks_of[0])
        kv = {(s, pr): (slabs(kt_ref, pr, blocks_of[s]), slabs(vt_ref, pr, blocks_of[s]))
              for s in range(n_sub) for pr in range(n_pair)}
        zb = [_dot(q_of(s, pr, h), kv[s, pr][0]) for s, pr, h in chains]
        sp = [jnp.maximum(z, 0.0) + jnp.log2(1.0 + jnp.exp2(_minus_abs(z))) for z in zb]
        if first_round:
            sp = [jnp.concatenate([x[:, :tq], jnp.where(causal, x[:, tq:], 0.0)], axis=1) for x in sp]
        hi = [x.astype(BF16) for x in sp]
        lo = [(x - y.astype(F32)).astype(BF16) for x, y in zip(sp, hi)]
        if first_round:
            after = _after_and_ones(width)[:, :width]
            later = [_dot(x, after) + _dot(y, after) for x, y in zip(hi, lo)]
            total = [jnp.broadcast_to(jnp.sum(x, axis=1, keepdims=True), (tq, LANES)) for x in sp]
        else:
            ao = _after_and_ones(width)
            ao2 = jnp.concatenate([ao, ao], axis=0)
            sums = [_dot(jnp.concatenate([x, y], axis=1), ao2) for x, y in zip(hi, lo)]
            later = [sm[:, :width] for sm in sums]
            total = [sm[:, width:] for sm in sums]
        e = [z - x - y for z, x, y in zip(zb, sp, later)]
        if first_round:
            dead_left = [jnp.where(first_block + i * n_sub + s == 0, DEAD, 0.0) for s, _, _ in chains]
            w = [jnp.concatenate([jnp.exp2(x[:, :tq] - d), jnp.where(causal, jnp.exp2(x[:, tq:]), 0.0)], axis=1)
                 for x, d in zip(e, dead_left)]
            run = total
        else:
            old = [run_ref[ch] for ch in chains]
            w = [jnp.exp2(x - o) for x, o in zip(e, old)]
            run = [o + t for o, t in zip(old, total)]
        for ch, x in zip(chains, w):
            acc_ref[ch] += lax.dot_general(x.astype(BF16), kv[ch[0], ch[1]][1], NT_DIMS,
                                           preferred_element_type=F32)
        for ch, x in zip(chains, run):
            run_ref[ch] = jnp.where(blocks_of[ch[0]][0] <= 0, DEAD, x)

    diag = [first_block + i * n_sub + s for s in range(n_sub)]
    run_round([[jnp.maximum(jd - 1, 0), jd] for jd in diag], True)

    def least_decay():
        return jnp.min(run_ref[...])

    def body(carry):
        rnd, _ = carry
        run_round([[jnp.maximum(jd - 1 - rnd, 0)] for jd in diag], False)
        return rnd + 1, least_decay()

    lax.while_loop(lambda carry: carry[1] < DECAY_DONE, body, (jnp.int32(1), least_decay()))

    for s in range(n_sub):
        for pr in range(n_pair):
            out = jnp.where(head_of_lane == 0, acc_ref[s, pr, 0], acc_ref[s, pr, 1])
            o_ref[s * tq:(s + 1) * tq, pr * LANES:(pr + 1) * LANES] = out.astype(o_ref.dtype)


def _attention(q, kt, vt, *, n_sub, n_pair):
    b, t_q, _ = q.shape
    n_blocks = kt.shape[1]
    tile = n_sub * TQ
    assert t_q % tile == 0 and (SB_WIDTH // LANES) % n_pair == 0
    assert kt.shape == vt.shape == (b, n_blocks, SB_WIDTH, TQ)
    wcol = n_pair * LANES
    slab_spec = pl.BlockSpec((None, n_blocks, wcol, TQ), lambda bi, hp, i: (bi, 0, hp, 0))
    return pl.pallas_call(
        functools.partial(_attn_kernel, n_sub=n_sub, n_pair=n_pair, first_block=n_blocks - t_q // TQ),
        grid=(b, SB_WIDTH // wcol, t_q // tile),
        in_specs=[pl.BlockSpec((None, tile, wcol), lambda bi, hp, i: (bi, i, hp)), slab_spec, slab_spec],
        out_specs=pl.BlockSpec((None, tile, wcol), lambda bi, hp, i: (bi, i, hp)),
        out_shape=jax.ShapeDtypeStruct((b, t_q, SB_WIDTH), BF16),
        scratch_shapes=[pltpu.VMEM((n_sub, n_pair, HEADS_PER_STEP, TQ, LANES), F32),
                        pltpu.VMEM((n_sub, n_pair, HEADS_PER_STEP, TQ, LANES), F32)],
        compiler_params=pltpu.CompilerParams(
            dimension_semantics=("arbitrary", "arbitrary", "arbitrary"),
            vmem_limit_bytes=VMEM_LIMIT_BYTES),
        name="attn",
    )(q, kt, vt)


def _conv_kernel(u_ref, halo_ref, hist_ref, cw_ref, cb_ref, lg_ref, lb_ref, o_ref, sh_ref):
    tm = u_ref.shape[0]
    pad = HALO_ROWS - (CONV_WIDTH - 1)
    sh_ref[0, 0:HALO_ROWS, :] = jnp.where(pl.program_id(1) == 0, hist_ref[...], halo_ref[...])
    sh_ref[0, HALO_ROWS:, :] = u_ref[...]
    for c0 in range(0, tm + HALO_ROWS - SUBLANES, SHIFT_ROWS):
        rows = min(SHIFT_ROWS, tm + HALO_ROWS - SUBLANES - c0)
        chunk = sh_ref[0, c0:c0 + rows + SUBLANES, :]
        for r in range(1, SUBLANES):
            sh_ref[r, c0:c0 + rows, :] = pltpu.roll(chunk, rows + SUBLANES - r, axis=0)[0:rows]
    for r0 in range(0, tm, CONV_ROWS):
        y = cb_ref[...]
        for t in range(CONV_WIDTH):
            r, a = (pad + t) % SUBLANES, (pad + t) // SUBLANES
            w_t = jnp.concatenate([cw_ref[t]] * (CONV_ROWS // SUBLANES), axis=0)
            y = y + w_t * sh_ref[r, r0 + SUBLANES * a:r0 + SUBLANES * a + CONV_ROWS, :]
        yc = y - jnp.mean(y, axis=-1, keepdims=True)
        yn = yc * lax.rsqrt(jnp.mean(yc * yc, axis=-1, keepdims=True) + EPS) * lg_ref[...] + lb_ref[...]
        o_ref[r0:r0 + CONV_ROWS, :] = (yn * jax.nn.sigmoid(yn)).astype(BF16)


def _conv(u, hist, cw, cb, lg, lb, *, tm):
    b, t, c = u.shape
    assert t % tm == 0 and tm % CONV_ROWS == 0
    halo_blocks = tm // HALO_ROWS
    weights = (cw, cb, lg, lb)
    return pl.pallas_call(
        _conv_kernel,
        grid=(b, t // tm),
        in_specs=[pl.BlockSpec((None, tm, c), lambda bi, j: (bi, j, 0)),
                  pl.BlockSpec((None, HALO_ROWS, c), lambda bi, j: (bi, jnp.maximum(j * halo_blocks - 1, 0), 0)),
                  pl.BlockSpec((None, HALO_ROWS, c), lambda bi, j: (bi, 0, 0))]
                 + [_resident(w.shape) for w in weights],
        out_specs=pl.BlockSpec((None, tm, c), lambda bi, j: (bi, j, 0)),
        out_shape=jax.ShapeDtypeStruct((b, t, c), BF16),
        scratch_shapes=[pltpu.VMEM((SUBLANES, HALO_ROWS + tm, c), F32)],
        compiler_params=pltpu.CompilerParams(
            dimension_semantics=("arbitrary", "arbitrary"), vmem_limit_bytes=VMEM_LIMIT_BYTES),
        name="conv",
    )(u, u, hist, *weights)


def _post_kernel(h1_ref, conv_ref, sb_ref, p_ref, wo_ref,
                 g2_ref, wgu_ref, wd_ref, gp_ref, wpg_ref, wp_ref, gf_ref, y_ref, *, fc):
    mixed = _dot(conv_ref[...], wo_ref[0:CONV_CH, :]) + _dot(sb_ref[...], wo_ref[CONV_CH:, :])
    h2 = h1_ref[...] + mixed
    n2 = _rms(h2, g2_ref[...]).astype(BF16)
    h3 = h2 + 0.5 * _swiglu(n2, wgu_ref, wd_ref, fc)
    n3 = _rms(h3, gp_ref[...]).astype(BF16)
    gate = jax.nn.sigmoid(_dot(n3, wpg_ref[...]))
    h4 = h3 + gate * _dot(p_ref[...].astype(BF16), wp_ref[...])
    y_ref[...] = _rms(h4, gf_ref[...])


def _post(h1, conv_out, sb, p, wo, g2, wgu, wd, gp, wpg, wp, gf, *, tm, fc):
    n, d = h1.shape
    row = lambda w: pl.BlockSpec((tm, w), lambda i: (i, 0))
    weights = (wo, g2, wgu, wd, gp, wpg, wp, gf)
    return pl.pallas_call(
        functools.partial(_post_kernel, fc=fc),
        grid=(n // tm,),
        in_specs=[row(d), row(CONV_CH), row(SB_WIDTH), row(p.shape[-1])]
                 + [_resident(w.shape) for w in weights],
        out_specs=row(d),
        out_shape=jax.ShapeDtypeStruct((n, d), F32),
        compiler_params=pltpu.CompilerParams(
            dimension_semantics=("arbitrary",), vmem_limit_bytes=VMEM_LIMIT_BYTES),
        name="post",
    )(h1, conv_out, sb, p, *weights)


def _layer(x, p, k_hist, v_hist, conv_hist, w, *, tm, tm_conv, n_sub, n_pair, fc):
    b, t, d = x.shape
    pre = functools.partial(_pre, g1=w["g1"], wgu=w["wgu1"], wd=w["wd1"], gm=w["gm"], win=w["win"],
                            wkvt=w["wkvt"], tm=tm, fc=fc)
    if k_hist is None:
        h1, u, q, kt, vt, ktb, vtb = pre(x)
        sb = _attention(q, ktb, vtb, n_sub=n_sub, n_pair=n_pair)
    else:
        h1, u, q, kt, vt, _, _ = pre(x.reshape(1, b * t, d))
        h1, u, q = (a.reshape(b, t, a.shape[-1]) for a in (h1, u, q))
        kt, vt = (a.reshape(SB_WIDTH, b, t).transpose(1, 0, 2) for a in (kt, vt))
        past = k_hist.shape[1]
        pad = (-t) % (n_sub * TQ)

        def slabs(hist, new):
            old = hist.transpose(0, 2, 3, 1).reshape(b, SB_WIDTH, past).astype(BF16)
            full = jnp.concatenate([old, new.astype(BF16), jnp.zeros((b, SB_WIDTH, pad), BF16)], axis=2)
            return full.reshape(b, SB_WIDTH, -1, TQ).transpose(0, 2, 1, 3)

        q = jnp.pad(q, ((0, 0), (0, pad), (0, 0)))
        sb = _attention(q, slabs(k_hist, kt), slabs(v_hist, vt), n_sub=n_sub, n_pair=n_pair)[:, :t]
    hist = jnp.zeros((b, HALO_ROWS, CONV_CH), F32)
    if conv_hist is not None:
        hist = hist.at[:, HALO_ROWS - (CONV_WIDTH - 1):, :].set(conv_hist)
    conv_out = _conv(u, hist, w["cw8"], w["cb"], w["lg"], w["lb"], tm=tm_conv)
    flat = lambda a: a.reshape(b * t, a.shape[-1])
    y = _post(flat(h1), flat(conv_out), flat(sb), flat(p),
              w["wo"], w["g2"], w["wgu2"], w["wd2"], w["gp"], w["wpg"], w["wp"], w["gf"],
              tm=min(512, b * t), fc=fc)
    assert t >= CONV_WIDTH - 1
    new_conv = u[:, t - (CONV_WIDTH - 1):, :]
    heads = lambda a: a.reshape(b, SB_HEADS, HEAD_DIM, t).transpose(0, 3, 1, 2)[None]
    return y.reshape(b, t, d), heads(kt), heads(vt), new_conv[None]


def kernel(x_prompt, x_sample, p_prompt, p_sample, cache_k, cache_v, state_conv, ffn1_norm, ffn1_w_gu, ffn1_w_down, mix_norm, w_in, conv_w, conv_b, conv_ln_g, conv_ln_b, w_out, ffn2_norm, ffn2_w_gu, ffn2_w_down, ple_norm, ple_gate_w, ple_w, final_norm):
    assert ffn1_norm.shape[0] == 1, "single-layer stack"
    row = lambda a: a.reshape(1, -1)
    w = dict(
        g1=ffn1_norm, wgu1=ffn1_w_gu[0].astype(BF16), wd1=ffn1_w_down[0].astype(BF16),
        gm=mix_norm, win=w_in[0][:, :2 * CONV_CH + SB_WIDTH].astype(BF16),
        wkvt=w_in[0][:, 2 * CONV_CH + SB_WIDTH:].T.astype(BF16),
        cw8=jnp.broadcast_to(conv_w[0][:, None, :], (CONV_WIDTH, SUBLANES, CONV_CH)),
        cb=conv_b, lg=conv_ln_g, lb=conv_ln_b, wo=w_out[0].astype(BF16),
        g2=ffn2_norm, wgu2=ffn2_w_gu[0].astype(BF16), wd2=ffn2_w_down[0].astype(BF16),
        gp=ple_norm, wpg=ple_gate_w[0].astype(BF16), wp=ple_w[0].astype(BF16), gf=row(final_norm))
    y_p, k_p, v_p, c_p = _layer(x_prompt, p_prompt[0], None, None, None, w,
                                tm=512, tm_conv=512, n_sub=16, n_pair=1, fc=256)
    y_s, k_s, v_s, c_s = _layer(x_sample, p_sample[0], cache_k[0], cache_v[0], state_conv[0], w,
                                tm=512, tm_conv=64, n_sub=1, n_pair=4, fc=256)
    return y_p, y_s, k_p, v_p, c_p, k_s, v_s, c_s
```

```python
import functools

import jax
import jax.numpy as jnp
from jax import lax
from jax.experimental import pallas as pl
from jax.experimental.pallas import tpu as pltpu

EPS = 1e-6
HEAD_DIM = 64
SB_HEADS = 8
SB_WIDTH = SB_HEADS * HEAD_DIM
CONV_CH = 512
CONV_WIDTH = 31
LOG2_E = 1.4426950408889634
Q_SCALE = LOG2_E / 8.0

LANES = 128
HEADS_PER_STEP = LANES // HEAD_DIM
TQ = LANES
HALO_ROWS = 32
SUBLANES = 8
CONV_ROWS = 32
SHIFT_ROWS = 64
VMEM_LIMIT_BYTES = 60 * 1024 * 1024

DECAY_DONE = 130.0
DEAD = 1e30

F32 = jnp.float32
BF16 = jnp.bfloat16
NT_DIMS = (((1,), (1,)), ((), ()))


def _rms(x, g):
    return x * lax.rsqrt(jnp.mean(x * x, axis=-1, keepdims=True) + EPS) * g


def _dot(a, b):
    return jnp.dot(a, b, preferred_element_type=F32)


def _swiglu(n, wgu_ref, wd_ref, fc):
    d_ff = wd_ref.shape[0]
    acc = None
    for c in range(d_ff // fc):
        g = _dot(n, wgu_ref[:, c * fc:(c + 1) * fc])
        u = _dot(n, wgu_ref[:, d_ff + c * fc:d_ff + (c + 1) * fc])
        a = (g * jax.nn.sigmoid(g) * u).astype(BF16)
        d = _dot(a, wd_ref[c * fc:(c + 1) * fc, :])
        acc = d if acc is None else acc + d
    return acc


def _pre_kernel(x_ref, g1_ref, wgu_ref, wd_ref, gm_ref, win_ref, wkvt_ref,
                h1_ref, u_ref, q_ref, kt_ref, vt_ref, ktb_ref, vtb_ref, *, fc):
    x = x_ref[...]
    n1 = _rms(x, g1_ref[...]).astype(BF16)
    h1 = x + 0.5 * _swiglu(n1, wgu_ref, wd_ref, fc)
    h1_ref[...] = h1
    n2 = _rms(h1, gm_ref[...]).astype(BF16)
    c = CONV_CH
    a1 = _dot(n2, win_ref[:, 0:c])
    a2 = _dot(n2, win_ref[:, c:2 * c])
    u_ref[...] = a1 * jax.nn.sigmoid(a2)
    q_ref[...] = (_dot(n2, win_ref[:, 2 * c:2 * c + SB_WIDTH]) * Q_SCALE).astype(BF16)
    kt = lax.dot_general(wkvt_ref[0:SB_WIDTH, :], n2, NT_DIMS, preferred_element_type=F32)
    vt = lax.dot_general(wkvt_ref[SB_WIDTH:, :], n2, NT_DIMS, preferred_element_type=F32)
    kt_ref[...] = kt
    vt_ref[...] = vt
    for jj in range(ktb_ref.shape[0]):
        ktb_ref[jj] = kt[:, jj * TQ:(jj + 1) * TQ].astype(BF16)
        vtb_ref[jj] = vt[:, jj * TQ:(jj + 1) * TQ].astype(BF16)


def _resident(shape):
    zeros = (0,) * len(shape)
    return pl.BlockSpec(shape, lambda *_: zeros, pipeline_mode=pl.Buffered(1))


def _pre(x, *, g1, wgu, wd, gm, win, wkvt, tm, fc):
    b, t, d = x.shape
    assert tm % TQ == 0 and t % tm == 0
    tile = lambda w: pl.BlockSpec((None, tm, w), lambda bi, j: (bi, j, 0))
    tmajor = pl.BlockSpec((None, SB_WIDTH, tm), lambda bi, j: (bi, 0, j))
    slabs = pl.BlockSpec((None, tm // TQ, SB_WIDTH, TQ), lambda bi, j: (bi, j, 0, 0))
    out_specs = [tile(d), tile(CONV_CH), tile(SB_WIDTH), tmajor, tmajor, slabs, slabs]
    out_shape = [jax.ShapeDtypeStruct((b, t, d), F32), jax.ShapeDtypeStruct((b, t, CONV_CH), F32),
                 jax.ShapeDtypeStruct((b, t, SB_WIDTH), BF16),
                 jax.ShapeDtypeStruct((b, SB_WIDTH, t), F32), jax.ShapeDtypeStruct((b, SB_WIDTH, t), F32),
                 jax.ShapeDtypeStruct((b, t // TQ, SB_WIDTH, TQ), BF16),
                 jax.ShapeDtypeStruct((b, t // TQ, SB_WIDTH, TQ), BF16)]
    weights = (g1, wgu, wd, gm, win, wkvt)
    return pl.pallas_call(
        functools.partial(_pre_kernel, fc=fc),
        grid=(b, t // tm),
        in_specs=[tile(d)] + [_resident(w.shape) for w in weights],
        out_specs=out_specs,
        out_shape=out_shape,
        compiler_params=pltpu.CompilerParams(
            dimension_semantics=("arbitrary", "arbitrary"), vmem_limit_bytes=VMEM_LIMIT_BYTES),
        name="pre",
    )(x, *weights)


def _after_and_ones(width):
    r = lax.broadcasted_iota(jnp.int32, (width, width + LANES), 0)
    c = lax.broadcasted_iota(jnp.int32, (width, width + LANES), 1)
    return jnp.logical_or(r > c, c >= width).astype(BF16)


def _minus_abs(x):
    bits = lax.bitcast_convert_type(x, jnp.uint32) | jnp.uint32(0x80000000)
    return lax.bitcast_convert_type(bits, F32)


def _attn_kernel(q_ref, kt_ref, vt_ref, o_ref, acc_ref, run_ref, *, n_sub, n_pair, first_block):
    i = pl.program_id(2)
    tq = TQ
    head_of_lane = lax.broadcasted_iota(jnp.int32, (tq, LANES), 1) // HEAD_DIM
    r = lax.broadcasted_iota(jnp.int32, (tq, tq), 0)
    c = lax.broadcasted_iota(jnp.int32, (tq, tq), 1)
    causal = c < r
    chains = [(s, pr, h) for s in range(n_sub) for pr in range(n_pair) for h in range(HEADS_PER_STEP)]

    acc_ref[...] = jnp.zeros_like(acc_ref)

    def q_of(s, pr, h):
        q2 = q_ref[s * tq:(s + 1) * tq, pr * LANES:(pr + 1) * LANES]
        return jnp.where(head_of_lane == h, q2, jnp.zeros_like(q2))

    def slabs(ref, pr, blocks):
        parts = [ref[jb, pr * LANES:(pr + 1) * LANES, :] for jb in blocks]
        return parts[0] if len(parts) == 1 else jnp.concatenate(parts, axis=1)

    def run_round(blocks_of, first_round):
        width = tq * len(blocks_of[0])
        kv = {(s, pr): (slabs(kt_ref, pr, blocks_of[s]), slabs(vt_ref, pr, blocks_of[s]))
              for s in range(n_sub) for pr in range(n_pair)}
        zb = [_dot(q_of(s, pr, h), kv[s, pr][0]) for s, pr, h in chains]
        sp = [jnp.maximum(z, 0.0) + jnp.log2(1.0 + jnp.exp2(_minus_abs(z))) for z in zb]
        if first_round:
            sp = [jnp.concatenate([x[:, :tq], jnp.where(causal, x[:, tq:], 0.0)], axis=1) for x in sp]
        hi = [x.astype(BF16) for x in sp]
        lo = [(x - y.astype(F32)).astype(BF16) for x, y in zip(sp, hi)]
        if first_round:
            after = _after_and_ones(width)[:, :width]
            later = [_dot(x, after) + _dot(y, after) for x, y in zip(hi, lo)]
            total = [jnp.broadcast_to(jnp.sum(x, axis=1, keepdims=True), (tq, LANES)) for x in sp]
        else:
            ao = _after_and_ones(width)
            ao2 = jnp.concatenate([ao, ao], axis=0)
            sums = [_dot(jnp.concatenate([x, y], axis=1), ao2) for x, y in zip(hi, lo)]
            later = [sm[:, :width] for sm in sums]
            total = [sm[:, width:] for sm in sums]
        e = [z - x - y for z, x, y in zip(zb, sp, later)]
        if first_round:
            dead_left = [jnp.where(first_block + i * n_sub + s == 0, DEAD, 0.0) for s, _, _ in chains]
            w = [jnp.concatenate([jnp.exp2(x[:, :tq] - d), jnp.where(causal, jnp.exp2(x[:, tq:]), 0.0)], axis=1)
                 for x, d in zip(e, dead_left)]
            run = total
        else:
            old = [run_ref[ch] for ch in chains]
            w = [jnp.exp2(x - o) for x, o in zip(e, old)]
            run = [o + t for o, t in zip(old, total)]
        for ch, x in zip(chains, w):
            acc_ref[ch] += lax.dot_general(x.astype(BF16), kv[ch[0], ch[1]][1], NT_DIMS,
                                           preferred_element_type=F32)
        for ch, x in zip(chains, run):
            run_ref[ch] = jnp.where(blocks_of[ch[0]][0] <= 0, DEAD, x)

    diag = [first_block + i * n_sub + s for s in range(n_sub)]
    run_round([[jnp.maximum(jd - 1, 0), jd] for jd in diag], True)

    def least_decay():
        return jnp.min(run_ref[...])

    def body(carry):
        rnd, _ = carry
        run_round([[jnp.maximum(jd - 1 - rnd, 0)] for jd in diag], False)
        return rnd + 1, least_decay()

    lax.while_loop(lambda carry: carry[1] < DECAY_DONE, body, (jnp.int32(1), least_decay()))

    for s in range(n_sub):
        for pr in range(n_pair):
            out = jnp.where(head_of_lane == 0, acc_ref[s, pr, 0], acc_ref[s, pr, 1])
            o_ref[s * tq:(s + 1) * tq, pr * LANES:(pr + 1) * LANES] = out.astype(o_ref.dtype)


def _attention(q, kt, vt, *, n_sub, n_pair, n_key_blocks=None):
    b, t_q, _ = q.shape
    n_blocks = kt.shape[1]
    n_key_blocks = n_blocks if n_key_blocks is None else n_key_blocks
    tile = n_sub * TQ
    assert t_q % tile == 0 and (SB_WIDTH // LANES) % n_pair == 0 and t_q // TQ <= n_key_blocks <= n_blocks
    assert kt.shape == vt.shape == (b, n_blocks, SB_WIDTH, TQ)
    wcol = n_pair * LANES
    slab_spec = pl.BlockSpec((None, n_blocks, wcol, TQ), lambda bi, hp, i: (bi, 0, hp, 0))
    return pl.pallas_call(
        functools.partial(_attn_kernel, n_sub=n_sub, n_pair=n_pair, first_block=n_key_blocks - t_q // TQ),
        grid=(b, SB_WIDTH // wcol, t_q // tile),
        in_specs=[pl.BlockSpec((None, tile, wcol), lambda bi, hp, i: (bi, i, hp)), slab_spec, slab_spec],
        out_specs=pl.BlockSpec((None, tile, wcol), lambda bi, hp, i: (bi, i, hp)),
        out_shape=jax.ShapeDtypeStruct((b, t_q, SB_WIDTH), BF16),
        scratch_shapes=[pltpu.VMEM((n_sub, n_pair, HEADS_PER_STEP, TQ, LANES), F32),
                        pltpu.VMEM((n_sub, n_pair, HEADS_PER_STEP, TQ, LANES), F32)],
        compiler_params=pltpu.CompilerParams(
            dimension_semantics=("arbitrary", "arbitrary", "arbitrary"),
            vmem_limit_bytes=VMEM_LIMIT_BYTES),
        name="attn",
    )(q, kt, vt)


def _slab_kernel(kc_ref, vc_ref, kn_ref, vn_ref, ko_ref, vo_ref):
    is_cache = pl.program_id(1) < pl.num_programs(1) - 1
    for cache_ref, new_ref, out_ref in ((kc_ref, kn_ref, ko_ref), (vc_ref, vn_ref, vo_ref)):
        x = jnp.where(is_cache, cache_ref[...], new_ref[...])
        for jj in range(out_ref.shape[0]):
            out_ref[jj] = x[:, jj * TQ:(jj + 1) * TQ].astype(BF16)


def _cached_slabs(kt_cache, vt_cache, kt_new, vt_new):
    b, feat, past = kt_cache.shape
    width = kt_new.shape[2]
    assert past % width == 0 and width % TQ == 0
    steps = past // width
    cache_spec = pl.BlockSpec((None, feat, width), lambda bi, j: (bi, 0, jnp.minimum(j, steps - 1)))
    new_spec = pl.BlockSpec((None, feat, width), lambda bi, j: (bi, 0, 0))
    out_spec = pl.BlockSpec((None, width // TQ, feat, TQ), lambda bi, j: (bi, j, 0, 0))
    out_shape = jax.ShapeDtypeStruct((b, (past + width) // TQ, feat, TQ), BF16)
    return pl.pallas_call(
        _slab_kernel,
        grid=(b, steps + 1),
        in_specs=[cache_spec, cache_spec, new_spec, new_spec],
        out_specs=[out_spec, out_spec],
        out_shape=[out_shape, out_shape],
        compiler_params=pltpu.CompilerParams(
            dimension_semantics=("arbitrary", "arbitrary"), vmem_limit_bytes=VMEM_LIMIT_BYTES),
        name="slabs",
    )(kt_cache, vt_cache, kt_new, vt_new)


def _conv_kernel(u_ref, halo_ref, hist_ref, cw_ref, cb_ref, lg_ref, lb_ref, o_ref, sh_ref):
    tm = u_ref.shape[0]
    pad = HALO_ROWS - (CONV_WIDTH - 1)
    sh_ref[0, 0:HALO_ROWS, :] = jnp.where(pl.program_id(1) == 0, hist_ref[...], halo_ref[...])
    sh_ref[0, HALO_ROWS:, :] = u_ref[...]
    for c0 in range(0, tm + HALO_ROWS - SUBLANES, SHIFT_ROWS):
        rows = min(SHIFT_ROWS, tm + HALO_ROWS - SUBLANES - c0)
        chunk = sh_ref[0, c0:c0 + rows + SUBLANES, :]
        for r in range(1, SUBLANES):
            sh_ref[r, c0:c0 + rows, :] = pltpu.roll(chunk, rows + SUBLANES - r, axis=0)[0:rows]
    for r0 in range(0, tm, CONV_ROWS):
        y = cb_ref[...]
        for t in range(CONV_WIDTH):
            r, a = (pad + t) % SUBLANES, (pad + t) // SUBLANES
            w_t = jnp.concatenate([cw_ref[t]] * (CONV_ROWS // SUBLANES), axis=0)
            y = y + w_t * sh_ref[r, r0 + SUBLANES * a:r0 + SUBLANES * a + CONV_ROWS, :]
        yc = y - jnp.mean(y, axis=-1, keepdims=True)
        yn = yc * lax.rsqrt(jnp.mean(yc * yc, axis=-1, keepdims=True) + EPS) * lg_ref[...] + lb_ref[...]
        o_ref[r0:r0 + CONV_ROWS, :] = (yn * jax.nn.sigmoid(yn)).astype(BF16)


def _conv(u, hist, cw, cb, lg, lb, *, tm):
    b, t, c = u.shape
    assert t % tm == 0 and tm % CONV_ROWS == 0
    halo_blocks = tm // HALO_ROWS
    weights = (cw, cb, lg, lb)
    return pl.pallas_call(
        _conv_kernel,
        grid=(b, t // tm),
        in_specs=[pl.BlockSpec((None, tm, c), lambda bi, j: (bi, j, 0)),
                  pl.BlockSpec((None, HALO_ROWS, c), lambda bi, j: (bi, jnp.maximum(j * halo_blocks - 1, 0), 0)),
                  pl.BlockSpec((None, HALO_ROWS, c), lambda bi, j: (bi, 0, 0))]
                 + [_resident(w.shape) for w in weights],
        out_specs=pl.BlockSpec((None, tm, c), lambda bi, j: (bi, j, 0)),
        out_shape=jax.ShapeDtypeStruct((b, t, c), BF16),
        scratch_shapes=[pltpu.VMEM((SUBLANES, HALO_ROWS + tm, c), F32)],
        compiler_params=pltpu.CompilerParams(
            dimension_semantics=("arbitrary", "arbitrary"), vmem_limit_bytes=VMEM_LIMIT_BYTES),
        name="conv",
    )(u, u, hist, *weights)


def _post_kernel(h1_ref, conv_ref, sb_ref, p_ref, wo_ref,
                 g2_ref, wgu_ref, wd_ref, gp_ref, wpg_ref, wp_ref, gf_ref, y_ref, *, fc):
    mixed = _dot(conv_ref[...], wo_ref[0:CONV_CH, :]) + _dot(sb_ref[...], wo_ref[CONV_CH:, :])
    h2 = h1_ref[...] + mixed
    n2 = _rms(h2, g2_ref[...]).astype(BF16)
    h3 = h2 + 0.5 * _swiglu(n2, wgu_ref, wd_ref, fc)
    n3 = _rms(h3, gp_ref[...]).astype(BF16)
    gate = jax.nn.sigmoid(_dot(n3, wpg_ref[...]))
    h4 = h3 + gate * _dot(p_ref[...].astype(BF16), wp_ref[...])
    y_ref[...] = _rms(h4, gf_ref[...])


def _post(h1, conv_out, sb, p, wo, g2, wgu, wd, gp, wpg, wp, gf, *, tm, fc):
    n, d = h1.shape
    row = lambda w: pl.BlockSpec((tm, w), lambda i: (i, 0))
    weights = (wo, g2, wgu, wd, gp, wpg, wp, gf)
    return pl.pallas_call(
        functools.partial(_post_kernel, fc=fc),
        grid=(n // tm,),
        in_specs=[row(d), row(CONV_CH), row(SB_WIDTH), row(p.shape[-1])]
                 + [_resident(w.shape) for w in weights],
        out_specs=row(d),
        out_shape=jax.ShapeDtypeStruct((n, d), F32),
        compiler_params=pltpu.CompilerParams(
            dimension_semantics=("arbitrary",), vmem_limit_bytes=VMEM_LIMIT_BYTES),
        name="post",
    )(h1, conv_out, sb, p, *weights)


def _layer(x, p, k_hist, v_hist, conv_hist, w, *, tm, tm_conv, n_sub, n_pair, fc):
    b, t, d = x.shape
    pre = functools.partial(_pre, g1=w["g1"], wgu=w["wgu1"], wd=w["wd1"], gm=w["gm"], win=w["win"],
                            wkvt=w["wkvt"], tm=tm, fc=fc)
    if k_hist is None:
        h1, u, q, kt, vt, ktb, vtb = pre(x)
        sb = _attention(q, ktb, vtb, n_sub=n_sub, n_pair=n_pair)
    else:
        h1, u, q, kt, vt, _, _ = pre(x.reshape(1, b * t, d))
        h1, u, q = (a.reshape(b, t, a.shape[-1]) for a in (h1, u, q))
        kt, vt = (a.reshape(SB_WIDTH, b, t).transpose(1, 0, 2) for a in (kt, vt))
        past = k_hist.shape[1]
        t_pad = -(-t // (n_sub * TQ)) * (n_sub * TQ)
        width = min(past, 4 * TQ)
        assert width >= t_pad
        time_minor = lambda a: a.transpose(0, 2, 3, 1).reshape(b, SB_WIDTH, past)
        widen = lambda a: jnp.pad(a, ((0, 0), (0, 0), (0, width - t)))
        ktb, vtb = _cached_slabs(time_minor(k_hist), time_minor(v_hist), widen(kt), widen(vt))
        q = jnp.pad(q, ((0, 0), (0, t_pad - t), (0, 0)))
        sb = _attention(q, ktb, vtb, n_sub=n_sub, n_pair=n_pair, n_key_blocks=(past + t_pad) // TQ)[:, :t]
    hist = jnp.zeros((b, HALO_ROWS, CONV_CH), F32)
    if conv_hist is not None:
        hist = hist.at[:, HALO_ROWS - (CONV_WIDTH - 1):, :].set(conv_hist)
    conv_out = _conv(u, hist, w["cw8"], w["cb"], w["lg"], w["lb"], tm=tm_conv)
    flat = lambda a: a.reshape(b * t, a.shape[-1])
    y = _post(flat(h1), flat(conv_out), flat(sb), flat(p),
              w["wo"], w["g2"], w["wgu2"], w["wd2"], w["gp"], w["wpg"], w["wp"], w["gf"],
              tm=min(512, b * t), fc=fc)
    assert t >= CONV_WIDTH - 1
    new_conv = u[:, t - (CONV_WIDTH - 1):, :]
    heads = lambda a: a.reshape(b, SB_HEADS, HEAD_DIM, t).transpose(0, 3, 1, 2)[None]
    return y.reshape(b, t, d), heads(kt), heads(vt), new_conv[None]


def kernel(x_prompt, x_sample, p_prompt, p_sample, cache_k, cache_v, state_conv, ffn1_norm, ffn1_w_gu, ffn1_w_down, mix_norm, w_in, conv_w, conv_b, conv_ln_g, conv_ln_b, w_out, ffn2_norm, ffn2_w_gu, ffn2_w_down, ple_norm, ple_gate_w, ple_w, final_norm):
    assert ffn1_norm.shape[0] == 1, "single-layer stack"
    row = lambda a: a.reshape(1, -1)
    w = dict(
        g1=ffn1_norm, wgu1=ffn1_w_gu[0].astype(BF16), wd1=ffn1_w_down[0].astype(BF16),
        gm=mix_norm, win=w_in[0][:, :2 * CONV_CH + SB_WIDTH].astype(BF16),
        wkvt=w_in[0][:, 2 * CONV_CH + SB_WIDTH:].T.astype(BF16),
        cw8=jnp.broadcast_to(conv_w[0][:, None, :], (CONV_WIDTH, SUBLANES, CONV_CH)),
        cb=conv_b, lg=conv_ln_g, lb=conv_ln_b, wo=w_out[0].astype(BF16),
        g2=ffn2_norm, wgu2=ffn2_w_gu[0].astype(BF16), wd2=ffn2_w_down[0].astype(BF16),
        gp=ple_norm, wpg=ple_gate_w[0].astype(BF16), wp=ple_w[0].astype(BF16), gf=row(final_norm))
    y_p, k_p, v_p, c_p = _layer(x_prompt, p_prompt[0], None, None, None, w,
                                tm=512, tm_conv=512, n_sub=8, n_pair=1, fc=256)
    y_s, k_s, v_s, c_s = _layer(x_sample, p_sample[0], cache_k[0], cache_v[0], state_conv[0], w,
                                tm=512, tm_conv=64, n_sub=1, n_pair=4, fc=256)
    return y_p, y_s, k_p, v_p, c_p, k_s, v_s, c_s
```

```python
import functools

import jax
import jax.numpy as jnp
from jax import lax
from jax.experimental import pallas as pl
from jax.experimental.pallas import tpu as pltpu

EPS = 1e-6
HEAD_DIM = 64
SB_HEADS = 8
SB_WIDTH = SB_HEADS * HEAD_DIM
CONV_CH = 512
CONV_WIDTH = 31
LOG2_E = 1.4426950408889634
Q_SCALE = LOG2_E / 8.0

LANES = 128
HEADS_PER_STEP = LANES // HEAD_DIM
TQ = LANES
HALO_ROWS = 32
SUBLANES = 8
CONV_ROWS = 32
SHIFT_ROWS = 64
VMEM_LIMIT_BYTES = 60 * 1024 * 1024

DECAY_DONE = 130.0
DEAD = 1e30

F32 = jnp.float32
BF16 = jnp.bfloat16
NT_DIMS = (((1,), (1,)), ((), ()))


def _rms(x, g):
    return x * lax.rsqrt(jnp.mean(x * x, axis=-1, keepdims=True) + EPS) * g


def _dot(a, b):
    return jnp.dot(a, b, preferred_element_type=F32)


def _zero_dependent_on(x):
    bits = lax.bitcast_convert_type(x, jnp.uint32)
    return lax.bitcast_convert_type((bits >> 16) >> 16, F32)


def _add_to_corner(x, piece):
    head = jnp.concatenate([x[0:SUBLANES, 0:LANES] + piece, x[0:SUBLANES, LANES:]], axis=1)
    return jnp.concatenate([head, x[SUBLANES:, :]], axis=0)


def _swiglu(n, wgu_ref, wd_ref, fc, side=()):
    d_ff = wd_ref.shape[0]
    n_chunks = d_ff // fc
    acc = None
    token = None
    for c in range(n_chunks):
        g = _dot(n, wgu_ref[:, c * fc:(c + 1) * fc])
        u = _dot(n, wgu_ref[:, d_ff + c * fc:d_ff + (c + 1) * fc])
        a = (g * jax.nn.sigmoid(g) * u).astype(BF16)
        d = _dot(a, wd_ref[c * fc:(c + 1) * fc, :])
        if token is not None:
            d = _add_to_corner(d, token)
            token = None
        acc = d if acc is None else acc + d
        for thunk in side[c * len(side) // n_chunks:(c + 1) * len(side) // n_chunks]:
            z = _zero_dependent_on(thunk())
            token = z if token is None else token + z
    return acc if token is None else _add_to_corner(acc, token)


def _pre_kernel(x_ref, g1_ref, wgu_ref, wd_ref, gm_ref, win_ref, wkvt_ref,
                h1_ref, u_ref, q_ref, kt_ref, vt_ref, ktb_ref, vtb_ref, *, fc):
    x = x_ref[...]
    n1 = _rms(x, g1_ref[...]).astype(BF16)
    h1 = x + 0.5 * _swiglu(n1, wgu_ref, wd_ref, fc)
    h1_ref[...] = h1
    n2 = _rms(h1, gm_ref[...]).astype(BF16)
    c = CONV_CH
    a1 = _dot(n2, win_ref[:, 0:c])
    a2 = _dot(n2, win_ref[:, c:2 * c])
    u_ref[...] = a1 * jax.nn.sigmoid(a2)
    q_ref[...] = (_dot(n2, win_ref[:, 2 * c:2 * c + SB_WIDTH]) * Q_SCALE).astype(BF16)
    kt = lax.dot_general(wkvt_ref[0:SB_WIDTH, :], n2, NT_DIMS, preferred_element_type=F32)
    vt = lax.dot_general(wkvt_ref[SB_WIDTH:, :], n2, NT_DIMS, preferred_element_type=F32)
    kt_ref[...] = kt
    vt_ref[...] = vt
    for jj in range(ktb_ref.shape[0]):
        ktb_ref[jj] = kt[:, jj * TQ:(jj + 1) * TQ].astype(BF16)
        vtb_ref[jj] = vt[:, jj * TQ:(jj + 1) * TQ].astype(BF16)


def _resident(shape):
    zeros = (0,) * len(shape)
    return pl.BlockSpec(shape, lambda *_: zeros, pipeline_mode=pl.Buffered(1))


def _pre(x, *, g1, wgu, wd, gm, win, wkvt, tm, fc):
    b, t, d = x.shape
    assert tm % TQ == 0 and t % tm == 0
    tile = lambda w: pl.BlockSpec((None, tm, w), lambda bi, j: (bi, j, 0))
    tmajor = pl.BlockSpec((None, SB_WIDTH, tm), lambda bi, j: (bi, 0, j))
    slabs = pl.BlockSpec((None, tm // TQ, SB_WIDTH, TQ), lambda bi, j: (bi, j, 0, 0))
    out_specs = [tile(d), tile(CONV_CH), tile(SB_WIDTH), tmajor, tmajor, slabs, slabs]
    out_shape = [jax.ShapeDtypeStruct((b, t, d), F32), jax.ShapeDtypeStruct((b, t, CONV_CH), F32),
                 jax.ShapeDtypeStruct((b, t, SB_WIDTH), BF16),
                 jax.ShapeDtypeStruct((b, SB_WIDTH, t), F32), jax.ShapeDtypeStruct((b, SB_WIDTH, t), F32),
                 jax.ShapeDtypeStruct((b, t // TQ, SB_WIDTH, TQ), BF16),
                 jax.ShapeDtypeStruct((b, t // TQ, SB_WIDTH, TQ), BF16)]
    weights = (g1, wgu, wd, gm, win, wkvt)
    return pl.pallas_call(
        functools.partial(_pre_kernel, fc=fc),
        grid=(b, t // tm),
        in_specs=[tile(d)] + [_resident(w.shape) for w in weights],
        out_specs=out_specs,
        out_shape=out_shape,
        compiler_params=pltpu.CompilerParams(
            dimension_semantics=("arbitrary", "arbitrary"), vmem_limit_bytes=VMEM_LIMIT_BYTES),
        name="pre",
    )(x, *weights)


def _after_and_ones(width):
    r = lax.broadcasted_iota(jnp.int32, (width, width + LANES), 0)
    c = lax.broadcasted_iota(jnp.int32, (width, width + LANES), 1)
    return jnp.logical_or(r > c, c >= width).astype(BF16)


def _minus_abs(x):
    bits = lax.bitcast_convert_type(x, jnp.uint32) | jnp.uint32(0x80000000)
    return lax.bitcast_convert_type(bits, F32)


def _attn_kernel(q_ref, kt_ref, vt_ref, o_ref, acc_ref, run_ref, *, n_sub, n_pair, first_block):
    i = pl.program_id(2)
    tq = TQ
    head_of_lane = lax.broadcasted_iota(jnp.int32, (tq, LANES), 1) // HEAD_DIM
    r = lax.broadcasted_iota(jnp.int32, (tq, tq), 0)
    c = lax.broadcasted_iota(jnp.int32, (tq, tq), 1)
    causal = c < r
    chains = [(s, pr, h) for s in range(n_sub) for pr in range(n_pair) for h in range(HEADS_PER_STEP)]

    acc_ref[...] = jnp.zeros_like(acc_ref)

    def q_of(s, pr, h):
        q2 = q_ref[s * tq:(s + 1) * tq, pr * LANES:(pr + 1) * LANES]
        return jnp.where(head_of_lane == h, q2, jnp.zeros_like(q2))

    def slabs(ref, pr, blocks):
        parts = [ref[jb, pr * LANES:(pr + 1) * LANES, :] for jb in blocks]
        return parts[0] if len(parts) == 1 else jnp.concatenate(parts, axis=1)

    def run_round(blocks_of, first_round):
        width = tq * len(blocks_of[0])
        kv = {(s, pr): (slabs(kt_ref, pr, blocks_of[s]), slabs(vt_ref, pr, blocks_of[s]))
              for s in range(n_sub) for pr in range(n_pair)}
        zb = [_dot(q_of(s, pr, h), kv[s, pr][0]) for s, pr, h in chains]
        sp = [jnp.maximum(z, 0.0) + jnp.log2(1.0 + jnp.exp2(_minus_abs(z))) for z in zb]
        if first_round:
            sp = [jnp.concatenate([x[:, :tq], jnp.where(causal, x[:, tq:], 0.0)], axis=1) for x in sp]
        hi = [x.astype(BF16) for x in sp]
        lo = [(x - y.astype(F32)).astype(BF16) for x, y in zip(sp, hi)]
        if first_round:
            after = _after_and_ones(width)[:, :width]
            later = [_dot(x, after) + _dot(y, after) for x, y in zip(hi, lo)]
            total = [jnp.broadcast_to(jnp.sum(x, axis=1, keepdims=True), (tq, LANES)) for x in sp]
        else:
            ao = _after_and_ones(width)
            ao2 = jnp.concatenate([ao, ao], axis=0)
            sums = [_dot(jnp.concatenate([x, y], axis=1), ao2) for x, y in zip(hi, lo)]
            later = [sm[:, :width] for sm in sums]
            total = [sm[:, width:] for sm in sums]
        e = [z - x - y for z, x, y in zip(zb, sp, later)]
        if first_round:
            dead_left = [jnp.where(first_block + i * n_sub + s == 0, DEAD, 0.0) for s, _, _ in chains]
            w = [jnp.concatenate([jnp.exp2(x[:, :tq] - d), jnp.where(causal, jnp.exp2(x[:, tq:]), 0.0)], axis=1)
                 for x, d in zip(e, dead_left)]
            run = total
        else:
            old = [run_ref[ch] for ch in chains]
            w = [jnp.exp2(x - o) for x, o in zip(e, old)]
            run = [o + t for o, t in zip(old, total)]
        for ch, x in zip(chains, w):
            acc_ref[ch] += lax.dot_general(x.astype(BF16), kv[ch[0], ch[1]][1], NT_DIMS,
                                           preferred_element_type=F32)
        for ch, x in zip(chains, run):
            run_ref[ch] = jnp.where(blocks_of[ch[0]][0] <= 0, DEAD, x)

    diag = [first_block + i * n_sub + s for s in range(n_sub)]
    run_round([[jnp.maximum(jd - 1, 0), jd] for jd in diag], True)

    def least_decay():
        return jnp.min(run_ref[...])

    def body(carry):
        rnd, _ = carry
        run_round([[jnp.maximum(jd - 1 - rnd, 0)] for jd in diag], False)
        return rnd + 1, least_decay()

    lax.while_loop(lambda carry: carry[1] < DECAY_DONE, body, (jnp.int32(1), least_decay()))

    for s in range(n_sub):
        for pr in range(n_pair):
            out = jnp.where(head_of_lane == 0, acc_ref[s, pr, 0], acc_ref[s, pr, 1])
            o_ref[s * tq:(s + 1) * tq, pr * LANES:(pr + 1) * LANES] = out.astype(o_ref.dtype)


def _attention(q, kt, vt, *, n_sub, n_pair, n_key_blocks=None):
    b, t_q, _ = q.shape
    n_blocks = kt.shape[1]
    n_key_blocks = n_blocks if n_key_blocks is None else n_key_blocks
    tile = n_sub * TQ
    assert t_q % tile == 0 and (SB_WIDTH // LANES) % n_pair == 0 and t_q // TQ <= n_key_blocks <= n_blocks
    assert kt.shape == vt.shape == (b, n_blocks, SB_WIDTH, TQ)
    wcol = n_pair * LANES
    slab_spec = pl.BlockSpec((None, n_blocks, wcol, TQ), lambda bi, hp, i: (bi, 0, hp, 0))
    return pl.pallas_call(
        functools.partial(_attn_kernel, n_sub=n_sub, n_pair=n_pair, first_block=n_key_blocks - t_q // TQ),
        grid=(b, SB_WIDTH // wcol, t_q // tile),
        in_specs=[pl.BlockSpec((None, tile, wcol), lambda bi, hp, i: (bi, i, hp)), slab_spec, slab_spec],
        out_specs=pl.BlockSpec((None, tile, wcol), lambda bi, hp, i: (bi, i, hp)),
        out_shape=jax.ShapeDtypeStruct((b, t_q, SB_WIDTH), BF16),
        scratch_shapes=[pltpu.VMEM((n_sub, n_pair, HEADS_PER_STEP, TQ, LANES), F32),
                        pltpu.VMEM((n_sub, n_pair, HEADS_PER_STEP, TQ, LANES), F32)],
        compiler_params=pltpu.CompilerParams(
            dimension_semantics=("arbitrary", "arbitrary", "arbitrary"),
            vmem_limit_bytes=VMEM_LIMIT_BYTES),
        name="attn",
    )(q, kt, vt)


def _slab_kernel(kc_ref, vc_ref, kn_ref, vn_ref, ko_ref, vo_ref):
    is_cache = pl.program_id(1) < pl.num_programs(1) - 1
    for cache_ref, new_ref, out_ref in ((kc_ref, kn_ref, ko_ref), (vc_ref, vn_ref, vo_ref)):
        x = jnp.where(is_cache, cache_ref[...], new_ref[...])
        for jj in range(out_ref.shape[0]):
            out_ref[jj] = x[:, jj * TQ:(jj + 1) * TQ].astype(BF16)


def _cached_slabs(kt_cache, vt_cache, kt_new, vt_new):
    b, feat, past = kt_cache.shape
    width = kt_new.shape[2]
    assert past % width == 0 and width % TQ == 0
    steps = past // width
    cache_spec = pl.BlockSpec((None, feat, width), lambda bi, j: (bi, 0, jnp.minimum(j, steps - 1)))
    new_spec = pl.BlockSpec((None, feat, width), lambda bi, j: (bi, 0, 0))
    out_spec = pl.BlockSpec((None, width // TQ, feat, TQ), lambda bi, j: (bi, j, 0, 0))
    out_shape = jax.ShapeDtypeStruct((b, (past + width) // TQ, feat, TQ), BF16)
    return pl.pallas_call(
        _slab_kernel,
        grid=(b, steps + 1),
        in_specs=[cache_spec, cache_spec, new_spec, new_spec],
        out_specs=[out_spec, out_spec],
        out_shape=[out_shape, out_shape],
        compiler_params=pltpu.CompilerParams(
            dimension_semantics=("arbitrary", "arbitrary"), vmem_limit_bytes=VMEM_LIMIT_BYTES),
        name="slabs",
    )(kt_cache, vt_cache, kt_new, vt_new)


def _conv_module(u_ref, left, out_ref, sh_ref, cw_ref, cb_ref, lg_ref, lb_ref):
    tm = u_ref.shape[0]
    pad = HALO_ROWS - (CONV_WIDTH - 1)
    sh_ref[0, 0:HALO_ROWS, :] = left
    sh_ref[0, HALO_ROWS:, :] = u_ref[...]

    def shift(c0):
        rows = min(SHIFT_ROWS, tm + HALO_ROWS - SUBLANES - c0)
        chunk = sh_ref[0, c0:c0 + rows + SUBLANES, :]
        for r in range(1, SUBLANES):
            rolled = pltpu.roll(chunk, rows + SUBLANES - r, axis=0)[0:rows]
            sh_ref[r, c0:c0 + rows, :] = rolled
        return rolled[0:SUBLANES, 0:LANES]

    def rows(r0):
        y = cb_ref[...]
        for t in range(CONV_WIDTH):
            r, a = (pad + t) % SUBLANES, (pad + t) // SUBLANES
            w_t = jnp.concatenate([cw_ref[t]] * (CONV_ROWS // SUBLANES), axis=0)
            y = y + w_t * sh_ref[r, r0 + SUBLANES * a:r0 + SUBLANES * a + CONV_ROWS, :]
        yc = y - jnp.mean(y, axis=-1, keepdims=True)
        yn = yc * lax.rsqrt(jnp.mean(yc * yc, axis=-1, keepdims=True) + EPS) * lg_ref[...] + lb_ref[...]
        out = yn * jax.nn.sigmoid(yn)
        out_ref[r0:r0 + CONV_ROWS, :] = out.astype(BF16)
        return out[0:SUBLANES, 0:LANES]

    return ([functools.partial(shift, c0) for c0 in range(0, tm + HALO_ROWS - SUBLANES, SHIFT_ROWS)]
            + [functools.partial(rows, r0) for r0 in range(0, tm, CONV_ROWS)])


def _post_kernel(h1_ref, u0_ref, hist0_ref, un_ref, halon_ref, histn_ref, sb_ref, p_ref,
                 cw_ref, cb_ref, lg_ref, lb_ref, wo_ref,
                 g2_ref, wgu_ref, wd_ref, gp_ref, wpg_ref, wp_ref, gf_ref,
                 y_ref, sh_ref, conv_ref, *, fc):
    conv = functools.partial(_conv_module, out_ref=conv_ref, sh_ref=sh_ref, cw_ref=cw_ref, cb_ref=cb_ref,
                             lg_ref=lg_ref, lb_ref=lb_ref)
    j = pl.program_id(1)

    @pl.when(jnp.logical_and(pl.program_id(0) == 0, j == 0))
    def _():
        for thunk in conv(u0_ref, hist0_ref[...]):
            thunk()

    mixed = _dot(conv_ref[...], wo_ref[0:CONV_CH, :]) + _dot(sb_ref[...], wo_ref[CONV_CH:, :])
    h2 = h1_ref[...] + mixed
    n2 = _rms(h2, g2_ref[...]).astype(BF16)
    opens_sequence = j == pl.num_programs(1) - 1
    next_conv = conv(un_ref, jnp.where(opens_sequence, histn_ref[...], halon_ref[...]))
    h3 = h2 + 0.5 * _swiglu(n2, wgu_ref, wd_ref, fc, side=next_conv)
    n3 = _rms(h3, gp_ref[...]).astype(BF16)
    gate = jax.nn.sigmoid(_dot(n3, wpg_ref[...]))
    h4 = h3 + gate * _dot(p_ref[...].astype(BF16), wp_ref[...])
    y_ref[...] = _rms(h4, gf_ref[...])


def _post(h1, u, hist, sb, p, cw, cb, lg, lb, wo, g2, wgu, wd, gp, wpg, wp, gf, *, tm, fc):
    b, t, d = h1.shape
    nj = t // tm
    assert t % tm == 0 and tm % CONV_ROWS == 0 and tm % HALO_ROWS == 0
    halo_blocks = tm // HALO_ROWS

    def nxt(bi, j):
        wrap = (j + 1 == nj).astype(jnp.int32)
        return jnp.minimum(bi + wrap, b - 1), (j + 1) * (1 - wrap)

    def halo_index(bi, j):
        bn, jn = nxt(bi, j)
        return bn, jnp.maximum(jn * halo_blocks - 1, 0), 0

    tile = lambda w: pl.BlockSpec((None, tm, w), lambda bi, j: (bi, j, 0))
    once = lambda r: pl.BlockSpec((None, r, CONV_CH), lambda bi, j: (0, 0, 0), pipeline_mode=pl.Buffered(1))
    weights = (cw, cb, lg, lb, wo, g2, wgu, wd, gp, wpg, wp, gf)
    return pl.pallas_call(
        functools.partial(_post_kernel, fc=fc),
        grid=(b, nj),
        in_specs=[tile(d), once(tm), once(HALO_ROWS),
                  pl.BlockSpec((None, tm, CONV_CH), lambda bi, j: (*nxt(bi, j), 0)),
                  pl.BlockSpec((None, HALO_ROWS, CONV_CH), halo_index),
                  pl.BlockSpec((None, HALO_ROWS, CONV_CH), lambda bi, j: (nxt(bi, j)[0], 0, 0)),
                  tile(SB_WIDTH), tile(p.shape[-1])]
                 + [_resident(w.shape) for w in weights],
        out_specs=tile(d),
        out_shape=jax.ShapeDtypeStruct((b, t, d), F32),
        scratch_shapes=[pltpu.VMEM((SUBLANES, HALO_ROWS + tm, CONV_CH), F32),
                        pltpu.VMEM((tm, CONV_CH), BF16)],
        compiler_params=pltpu.CompilerParams(
            dimension_semantics=("arbitrary", "arbitrary"), vmem_limit_bytes=VMEM_LIMIT_BYTES),
        name="post",
    )(h1, u, hist, u, u, hist, sb, p, *weights)


def _layer(x, p, k_hist, v_hist, conv_hist, w, *, tm, tm_conv, n_sub, n_pair, fc):
    b, t, d = x.shape
    pre = functools.partial(_pre, g1=w["g1"], wgu=w["wgu1"], wd=w["wd1"], gm=w["gm"], win=w["win"],
                            wkvt=w["wkvt"], tm=tm, fc=fc)
    if k_hist is None:
        h1, u, q, kt, vt, ktb, vtb = pre(x)
        sb = _attention(q, ktb, vtb, n_sub=n_sub, n_pair=n_pair)
    else:
        h1, u, q, kt, vt, _, _ = pre(x.reshape(1, b * t, d))
        h1, u, q = (a.reshape(b, t, a.shape[-1]) for a in (h1, u, q))
        kt, vt = (a.reshape(SB_WIDTH, b, t).transpose(1, 0, 2) for a in (kt, vt))
        past = k_hist.shape[1]
        t_pad = -(-t // (n_sub * TQ)) * (n_sub * TQ)
        width = min(past, 4 * TQ)
        assert width >= t_pad
        time_minor = lambda a: a.transpose(0, 2, 3, 1).reshape(b, SB_WIDTH, past)
        widen = lambda a: jnp.pad(a, ((0, 0), (0, 0), (0, width - t)))
        ktb, vtb = _cached_slabs(time_minor(k_hist), time_minor(v_hist), widen(kt), widen(vt))
        q = jnp.pad(q, ((0, 0), (0, t_pad - t), (0, 0)))
        sb = _attention(q, ktb, vtb, n_sub=n_sub, n_pair=n_pair, n_key_blocks=(past + t_pad) // TQ)[:, :t]
    hist = jnp.zeros((b, HALO_ROWS, CONV_CH), F32)
    if conv_hist is not None:
        hist = hist.at[:, HALO_ROWS - (CONV_WIDTH - 1):, :].set(conv_hist)
    y = _post(h1, u, hist, sb, p, w["cw8"], w["cb"], w["lg"], w["lb"],
              w["wo"], w["g2"], w["wgu2"], w["wd2"], w["gp"], w["wpg"], w["wp"], w["gf"], tm=tm_conv, fc=fc)
    assert t >= CONV_WIDTH - 1
    new_conv = u[:, t - (CONV_WIDTH - 1):, :]
    heads = lambda a: a.reshape(b, SB_HEADS, HEAD_DIM, t).transpose(0, 3, 1, 2)[None]
    return y, heads(kt), heads(vt), new_conv[None]


def kernel(x_prompt, x_sample, p_prompt, p_sample, cache_k, cache_v, state_conv, ffn1_norm, ffn1_w_gu, ffn1_w_down, mix_norm, w_in, conv_w, conv_b, conv_ln_g, conv_ln_b, w_out, ffn2_norm, ffn2_w_gu, ffn2_w_down, ple_norm, ple_gate_w, ple_w, final_norm):
    assert ffn1_norm.shape[0] == 1, "single-layer stack"
    row = lambda a: a.reshape(1, -1)
    w = dict(
        g1=ffn1_norm, wgu1=ffn1_w_gu[0].astype(BF16), wd1=ffn1_w_down[0].astype(BF16),
        gm=mix_norm, win=w_in[0][:, :2 * CONV_CH + SB_WIDTH].astype(BF16),
        wkvt=w_in[0][:, 2 * CONV_CH + SB_WIDTH:].T.astype(BF16),
        cw8=jnp.broadcast_to(conv_w[0][:, None, :], (CONV_WIDTH, SUBLANES, CONV_CH)),
        cb=conv_b, lg=conv_ln_g, lb=conv_ln_b, wo=w_out[0].astype(BF16),
        g2=ffn2_norm, wgu2=ffn2_w_gu[0].astype(BF16), wd2=ffn2_w_down[0].astype(BF16),
        gp=ple_norm, wpg=ple_gate_w[0].astype(BF16), wp=ple_w[0].astype(BF16), gf=row(final_norm))
    y_p, k_p, v_p, c_p = _layer(x_prompt, p_prompt[0], None, None, None, w,
                                tm=512, tm_conv=512, n_sub=8, n_pair=1, fc=256)
    y_s, k_s, v_s, c_s = _layer(x_sample, p_sample[0], cache_k[0], cache_v[0], state_conv[0], w,
                                tm=512, tm_conv=64, n_sub=1, n_pair=4, fc=256)
    return y_p, y_s, k_p, v_p, c_p, k_s, v_s, c_s
```

```python
import functools

import jax
import jax.numpy as jnp
from jax import lax
from jax.experimental import pallas as pl
from jax.experimental.pallas import tpu as pltpu

EPS = 1e-6
HEAD_DIM = 64
SB_HEADS = 8
SB_WIDTH = SB_HEADS * HEAD_DIM
CONV_CH = 512
CONV_WIDTH = 31
LOG2_E = 1.4426950408889634
Q_SCALE = LOG2_E / 8.0

LANES = 128
HEADS_PER_STEP = LANES // HEAD_DIM
TQ = LANES
HALO_ROWS = 32
SUBLANES = 8
CONV_ROWS = 32
SHIFT_ROWS = 64
VMEM_LIMIT_BYTES = 60 * 1024 * 1024

DECAY_DONE = 130.0
DEAD = 1e30

F32 = jnp.float32
BF16 = jnp.bfloat16
NT_DIMS = (((1,), (1,)), ((), ()))


def _rms(x, g):
    return x * lax.rsqrt(jnp.mean(x * x, axis=-1, keepdims=True) + EPS) * g


def _dot(a, b):
    return jnp.dot(a, b, preferred_element_type=F32)


def _zero_dependent_on(x):
    bits = lax.bitcast_convert_type(x, jnp.uint32)
    return lax.bitcast_convert_type((bits >> 16) >> 16, F32)


def _add_to_corner(x, piece):
    head = jnp.concatenate([x[0:SUBLANES, 0:LANES] + piece, x[0:SUBLANES, LANES:]], axis=1)
    return jnp.concatenate([head, x[SUBLANES:, :]], axis=0)


def _swiglu(n, wgu_ref, wd_ref, fc, side=()):
    d_ff = wd_ref.shape[0]
    n_chunks = d_ff // fc
    acc = None
    token = None
    for c in range(n_chunks):
        g = _dot(n, wgu_ref[:, c * fc:(c + 1) * fc])
        u = _dot(n, wgu_ref[:, d_ff + c * fc:d_ff + (c + 1) * fc])
        a = g * jax.nn.sigmoid(g) * u
        if token is not None:
            a = _add_to_corner(a, token)
            token = None
        d = _dot(a.astype(BF16), wd_ref[c * fc:(c + 1) * fc, :])
        acc = d if acc is None else acc + d
        for thunk in side[c * len(side) // n_chunks:(c + 1) * len(side) // n_chunks]:
            z = _zero_dependent_on(thunk())
            token = z if token is None else token + z
    return acc if token is None else _add_to_corner(acc, token)


def _pre_kernel(x_ref, g1_ref, wgu_ref, wd_ref, gm_ref, win_ref, wkvt_ref,
                h1_ref, u_ref, q_ref, kt_ref, vt_ref, ktb_ref, vtb_ref, *, fc):
    x = x_ref[...]
    n1 = _rms(x, g1_ref[...]).astype(BF16)
    h1 = x + 0.5 * _swiglu(n1, wgu_ref, wd_ref, fc)
    h1_ref[...] = h1
    n2 = _rms(h1, gm_ref[...]).astype(BF16)
    c = CONV_CH
    a1 = _dot(n2, win_ref[:, 0:c])
    a2 = _dot(n2, win_ref[:, c:2 * c])
    u_ref[...] = a1 * jax.nn.sigmoid(a2)
    q_ref[...] = (_dot(n2, win_ref[:, 2 * c:2 * c + SB_WIDTH]) * Q_SCALE).astype(BF16)
    kt = lax.dot_general(wkvt_ref[0:SB_WIDTH, :], n2, NT_DIMS, preferred_element_type=F32)
    vt = lax.dot_general(wkvt_ref[SB_WIDTH:, :], n2, NT_DIMS, preferred_element_type=F32)
    kt_ref[...] = kt
    vt_ref[...] = vt
    for jj in range(ktb_ref.shape[0]):
        ktb_ref[jj] = kt[:, jj * TQ:(jj + 1) * TQ].astype(BF16)
        vtb_ref[jj] = vt[:, jj * TQ:(jj + 1) * TQ].astype(BF16)


def _resident(shape):
    zeros = (0,) * len(shape)
    return pl.BlockSpec(shape, lambda *_: zeros, pipeline_mode=pl.Buffered(1))


def _pre(x, *, g1, wgu, wd, gm, win, wkvt, tm, fc):
    b, t, d = x.shape
    assert tm % TQ == 0 and t % tm == 0
    tile = lambda w: pl.BlockSpec((None, tm, w), lambda bi, j: (bi, j, 0))
    tmajor = pl.BlockSpec((None, SB_WIDTH, tm), lambda bi, j: (bi, 0, j))
    slabs = pl.BlockSpec((None, tm // TQ, SB_WIDTH, TQ), lambda bi, j: (bi, j, 0, 0))
    out_specs = [tile(d), tile(CONV_CH), tile(SB_WIDTH), tmajor, tmajor, slabs, slabs]
    out_shape = [jax.ShapeDtypeStruct((b, t, d), F32), jax.ShapeDtypeStruct((b, t, CONV_CH), F32),
                 jax.ShapeDtypeStruct((b, t, SB_WIDTH), BF16),
                 jax.ShapeDtypeStruct((b, SB_WIDTH, t), F32), jax.ShapeDtypeStruct((b, SB_WIDTH, t), F32),
                 jax.ShapeDtypeStruct((b, t // TQ, SB_WIDTH, TQ), BF16),
                 jax.ShapeDtypeStruct((b, t // TQ, SB_WIDTH, TQ), BF16)]
    weights = (g1, wgu, wd, gm, win, wkvt)
    return pl.pallas_call(
        functools.partial(_pre_kernel, fc=fc),
        grid=(b, t // tm),
        in_specs=[tile(d)] + [_resident(w.shape) for w in weights],
        out_specs=out_specs,
        out_shape=out_shape,
        compiler_params=pltpu.CompilerParams(
            dimension_semantics=("arbitrary", "arbitrary"), vmem_limit_bytes=VMEM_LIMIT_BYTES),
        name="pre",
    )(x, *weights)


def _after_and_ones(width):
    r = lax.broadcasted_iota(jnp.int32, (width, width + LANES), 0)
    c = lax.broadcasted_iota(jnp.int32, (width, width + LANES), 1)
    return jnp.logical_or(r > c, c >= width).astype(BF16)


def _minus_abs(x):
    bits = lax.bitcast_convert_type(x, jnp.uint32) | jnp.uint32(0x80000000)
    return lax.bitcast_convert_type(bits, F32)


def _attn_kernel(q_ref, kt_ref, vt_ref, o_ref, acc_ref, run_ref, *, n_sub, n_pair, first_block):
    i = pl.program_id(2)
    tq = TQ
    head_of_lane = lax.broadcasted_iota(jnp.int32, (tq, LANES), 1) // HEAD_DIM
    r = lax.broadcasted_iota(jnp.int32, (tq, tq), 0)
    c = lax.broadcasted_iota(jnp.int32, (tq, tq), 1)
    causal = c < r
    chains = [(s, pr, h) for s in range(n_sub) for pr in range(n_pair) for h in range(HEADS_PER_STEP)]

    acc_ref[...] = jnp.zeros_like(acc_ref)

    def q_of(s, pr, h):
        q2 = q_ref[s * tq:(s + 1) * tq, pr * LANES:(pr + 1) * LANES]
        return jnp.where(head_of_lane == h, q2, jnp.zeros_like(q2))

    def slabs(ref, pr, blocks):
        parts = [ref[jb, pr * LANES:(pr + 1) * LANES, :] for jb in blocks]
        return parts[0] if len(parts) == 1 else jnp.concatenate(parts, axis=1)

    def run_round(blocks_of, first_round):
        width = tq * len(blocks_of[0])
        kv = {(s, pr): (slabs(kt_ref, pr, blocks_of[s]), slabs(vt_ref, pr, blocks_of[s]))
              for s in range(n_sub) for pr in range(n_pair)}
        zb = [_dot(q_of(s, pr, h), kv[s, pr][0]) for s, pr, h in chains]
        sp = [jnp.maximum(z, 0.0) + jnp.log2(1.0 + jnp.exp2(_minus_abs(z))) for z in zb]
        if first_round:
            sp = [jnp.concatenate([x[:, :tq], jnp.where(causal, x[:, tq:], 0.0)], axis=1) for x in sp]
        hi = [x.astype(BF16) for x in sp]
        lo = [(x - y.astype(F32)).astype(BF16) for x, y in zip(sp, hi)]
        if first_round:
            after = _after_and_ones(width)[:, :width]
            later = [_dot(x, after) + _dot(y, after) for x, y in zip(hi, lo)]
            total = [jnp.broadcast_to(jnp.sum(x, axis=1, keepdims=True), (tq, LANES)) for x in sp]
        else:
            ao = _after_and_ones(width)
            ao2 = jnp.concatenate([ao, ao], axis=0)
            sums = [_dot(jnp.concatenate([x, y], axis=1), ao2) for x, y in zip(hi, lo)]
            later = [sm[:, :width] for sm in sums]
            total = [sm[:, width:] for sm in sums]
        e = [z - x - y for z, x, y in zip(zb, sp, later)]
        if first_round:
            dead_left = [jnp.where(first_block + i * n_sub + s == 0, DEAD, 0.0) for s, _, _ in chains]
            w = [jnp.concatenate([jnp.exp2(x[:, :tq] - d), jnp.where(causal, jnp.exp2(x[:, tq:]), 0.0)], axis=1)
                 for x, d in zip(e, dead_left)]
            run = total
        else:
            old = [run_ref[ch] for ch in chains]
            w = [jnp.exp2(x - o) for x, o in zip(e, old)]
            run = [o + t for o, t in zip(old, total)]
        for ch, x in zip(chains, w):
            acc_ref[ch] += lax.dot_general(x.astype(BF16), kv[ch[0], ch[1]][1], NT_DIMS,
                                           preferred_element_type=F32)
        for ch, x in zip(chains, run):
            run_ref[ch] = jnp.where(blocks_of[ch[0]][0] <= 0, DEAD, x)

    diag = [first_block + i * n_sub + s for s in range(n_sub)]
    run_round([[jnp.maximum(jd - 1, 0), jd] for jd in diag], True)

    def least_decay():
        return jnp.min(run_ref[...])

    def body(carry):
        rnd, _ = carry
        run_round([[jnp.maximum(jd - 1 - rnd, 0)] for jd in diag], False)
        return rnd + 1, least_decay()

    lax.while_loop(lambda carry: carry[1] < DECAY_DONE, body, (jnp.int32(1), least_decay()))

    for s in range(n_sub):
        for pr in range(n_pair):
            out = jnp.where(head_of_lane == 0, acc_ref[s, pr, 0], acc_ref[s, pr, 1])
            o_ref[s * tq:(s + 1) * tq, pr * LANES:(pr + 1) * LANES] = out.astype(o_ref.dtype)


def _attention(q, kt, vt, *, n_sub, n_pair, n_key_blocks=None):
    b, t_q, _ = q.shape
    n_blocks = kt.shape[1]
    n_key_blocks = n_blocks if n_key_blocks is None else n_key_blocks
    tile = n_sub * TQ
    assert t_q % tile == 0 and (SB_WIDTH // LANES) % n_pair == 0 and t_q // TQ <= n_key_blocks <= n_blocks
    assert kt.shape == vt.shape == (b, n_blocks, SB_WIDTH, TQ)
    wcol = n_pair * LANES
    slab_spec = pl.BlockSpec((None, n_blocks, wcol, TQ), lambda bi, hp, i: (bi, 0, hp, 0))
    return pl.pallas_call(
        functools.partial(_attn_kernel, n_sub=n_sub, n_pair=n_pair, first_block=n_key_blocks - t_q // TQ),
        grid=(b, SB_WIDTH // wcol, t_q // tile),
        in_specs=[pl.BlockSpec((None, tile, wcol), lambda bi, hp, i: (bi, i, hp)), slab_spec, slab_spec],
        out_specs=pl.BlockSpec((None, tile, wcol), lambda bi, hp, i: (bi, i, hp)),
        out_shape=jax.ShapeDtypeStruct((b, t_q, SB_WIDTH), BF16),
        scratch_shapes=[pltpu.VMEM((n_sub, n_pair, HEADS_PER_STEP, TQ, LANES), F32),
                        pltpu.VMEM((n_sub, n_pair, HEADS_PER_STEP, TQ, LANES), F32)],
        compiler_params=pltpu.CompilerParams(
            dimension_semantics=("arbitrary", "arbitrary", "arbitrary"),
            vmem_limit_bytes=VMEM_LIMIT_BYTES),
        name="attn",
    )(q, kt, vt)


def _slab_kernel(kc_ref, vc_ref, kn_ref, vn_ref, ko_ref, vo_ref):
    is_cache = pl.program_id(1) < pl.num_programs(1) - 1
    for cache_ref, new_ref, out_ref in ((kc_ref, kn_ref, ko_ref), (vc_ref, vn_ref, vo_ref)):
        x = jnp.where(is_cache, cache_ref[...], new_ref[...])
        for jj in range(out_ref.shape[0]):
            out_ref[jj] = x[:, jj * TQ:(jj + 1) * TQ].astype(BF16)


def _cached_slabs(kt_cache, vt_cache, kt_new, vt_new):
    b, feat, past = kt_cache.shape
    width = kt_new.shape[2]
    assert past % width == 0 and width % TQ == 0
    steps = past // width
    cache_spec = pl.BlockSpec((None, feat, width), lambda bi, j: (bi, 0, jnp.minimum(j, steps - 1)))
    new_spec = pl.BlockSpec((None, feat, width), lambda bi, j: (bi, 0, 0))
    out_spec = pl.BlockSpec((None, width // TQ, feat, TQ), lambda bi, j: (bi, j, 0, 0))
    out_shape = jax.ShapeDtypeStruct((b, (past + width) // TQ, feat, TQ), BF16)
    return pl.pallas_call(
        _slab_kernel,
        grid=(b, steps + 1),
        in_specs=[cache_spec, cache_spec, new_spec, new_spec],
        out_specs=[out_spec, out_spec],
        out_shape=[out_shape, out_shape],
        compiler_params=pltpu.CompilerParams(
            dimension_semantics=("arbitrary", "arbitrary"), vmem_limit_bytes=VMEM_LIMIT_BYTES),
        name="slabs",
    )(kt_cache, vt_cache, kt_new, vt_new)


def _conv_module(u_ref, left, out_ref, sh_ref, cw_ref, cb_ref, lg_ref, lb_ref):
    tm = u_ref.shape[0]
    pad = HALO_ROWS - (CONV_WIDTH - 1)
    sh_ref[0, 0:HALO_ROWS, :] = left
    sh_ref[0, HALO_ROWS:, :] = u_ref[...]

    def shift(c0):
        rows = min(SHIFT_ROWS, tm + HALO_ROWS - SUBLANES - c0)
        chunk = sh_ref[0, c0:c0 + rows + SUBLANES, :]
        for r in range(1, SUBLANES):
            rolled = pltpu.roll(chunk, rows + SUBLANES - r, axis=0)[0:rows]
            sh_ref[r, c0:c0 + rows, :] = rolled
        return rolled[0:SUBLANES, 0:LANES]

    def rows(r0):
        y = cb_ref[...]
        for t in range(CONV_WIDTH):
            r, a = (pad + t) % SUBLANES, (pad + t) // SUBLANES
            w_t = jnp.concatenate([cw_ref[t]] * (CONV_ROWS // SUBLANES), axis=0)
            y = y + w_t * sh_ref[r, r0 + SUBLANES * a:r0 + SUBLANES * a + CONV_ROWS, :]
        yc = y - jnp.mean(y, axis=-1, keepdims=True)
        yn = yc * lax.rsqrt(jnp.mean(yc * yc, axis=-1, keepdims=True) + EPS) * lg_ref[...] + lb_ref[...]
        out = yn * jax.nn.sigmoid(yn)
        out_ref[r0:r0 + CONV_ROWS, :] = out.astype(BF16)
        return out[0:SUBLANES, 0:LANES]

    return ([functools.partial(shift, c0) for c0 in range(0, tm + HALO_ROWS - SUBLANES, SHIFT_ROWS)]
            + [functools.partial(rows, r0) for r0 in range(0, tm, CONV_ROWS)])


def _post_kernel(h1_ref, u0_ref, hist0_ref, un_ref, halon_ref, histn_ref, sb_ref, p_ref,
                 cw_ref, cb_ref, lg_ref, lb_ref, wo_ref,
                 g2_ref, wgu_ref, wd_ref, gp_ref, wpg_ref, wp_ref, gf_ref,
                 y_ref, sh_ref, conv_ref, *, fc):
    conv = functools.partial(_conv_module, out_ref=conv_ref, sh_ref=sh_ref, cw_ref=cw_ref, cb_ref=cb_ref,
                             lg_ref=lg_ref, lb_ref=lb_ref)
    j = pl.program_id(1)

    @pl.when(jnp.logical_and(pl.program_id(0) == 0, j == 0))
    def _():
        for thunk in conv(u0_ref, hist0_ref[...]):
            thunk()

    opens_sequence = j == pl.num_programs(1) - 1
    next_conv = conv(un_ref, jnp.where(opens_sequence, histn_ref[...], halon_ref[...]))
    n_shift = len(next_conv) - h1_ref.shape[0] // CONV_ROWS
    token = None
    for thunk in next_conv[:n_shift]:
        z = _zero_dependent_on(thunk())
        token = z if token is None else token + z
    mixed = _dot(conv_ref[...], wo_ref[0:CONV_CH, :]) + _dot(sb_ref[...], wo_ref[CONV_CH:, :])
    h2 = h1_ref[...] + _add_to_corner(mixed, token)
    n2 = _rms(h2, g2_ref[...]).astype(BF16)
    h3 = h2 + 0.5 * _swiglu(n2, wgu_ref, wd_ref, fc, side=next_conv[n_shift:])
    n3 = _rms(h3, gp_ref[...]).astype(BF16)
    gate = jax.nn.sigmoid(_dot(n3, wpg_ref[...]))
    h4 = h3 + gate * _dot(p_ref[...].astype(BF16), wp_ref[...])
    y_ref[...] = _rms(h4, gf_ref[...])


def _post(h1, u, hist, sb, p, cw, cb, lg, lb, wo, g2, wgu, wd, gp, wpg, wp, gf, *, tm, fc):
    b, t, d = h1.shape
    nj = t // tm
    assert t % tm == 0 and tm % CONV_ROWS == 0 and tm % HALO_ROWS == 0
    halo_blocks = tm // HALO_ROWS

    def nxt(bi, j):
        wrap = (j + 1 == nj).astype(jnp.int32)
        return jnp.minimum(bi + wrap, b - 1), (j + 1) * (1 - wrap)

    def halo_index(bi, j):
        bn, jn = nxt(bi, j)
        return bn, jnp.maximum(jn * halo_blocks - 1, 0), 0

    tile = lambda w: pl.BlockSpec((None, tm, w), lambda bi, j: (bi, j, 0))
    once = lambda r: pl.BlockSpec((None, r, CONV_CH), lambda bi, j: (0, 0, 0), pipeline_mode=pl.Buffered(1))
    weights = (cw, cb, lg, lb, wo, g2, wgu, wd, gp, wpg, wp, gf)
    return pl.pallas_call(
        functools.partial(_post_kernel, fc=fc),
        grid=(b, nj),
        in_specs=[tile(d), once(tm), once(HALO_ROWS),
                  pl.BlockSpec((None, tm, CONV_CH), lambda bi, j: (*nxt(bi, j), 0)),
                  pl.BlockSpec((None, HALO_ROWS, CONV_CH), halo_index),
                  pl.BlockSpec((None, HALO_ROWS, CONV_CH), lambda bi, j: (nxt(bi, j)[0], 0, 0)),
                  tile(SB_WIDTH), tile(p.shape[-1])]
                 + [_resident(w.shape) for w in weights],
        out_specs=tile(d),
        out_shape=jax.ShapeDtypeStruct((b, t, d), F32),
        scratch_shapes=[pltpu.VMEM((SUBLANES, HALO_ROWS + tm, CONV_CH), F32),
                        pltpu.VMEM((tm, CONV_CH), BF16)],
        compiler_params=pltpu.CompilerParams(
            dimension_semantics=("arbitrary", "arbitrary"), vmem_limit_bytes=VMEM_LIMIT_BYTES),
        name="post",
    )(h1, u, hist, u, u, hist, sb, p, *weights)


def _layer(x, p, k_hist, v_hist, conv_hist, w, *, tm, tm_conv, n_sub, n_pair, fc):
    b, t, d = x.shape
    pre = functools.partial(_pre, g1=w["g1"], wgu=w["wgu1"], wd=w["wd1"], gm=w["gm"], win=w["win"],
                            wkvt=w["wkvt"], tm=tm, fc=fc)
    if k_hist is None:
        h1, u, q, kt, vt, ktb, vtb = pre(x)
        sb = _attention(q, ktb, vtb, n_sub=n_sub, n_pair=n_pair)
    else:
        h1, u, q, kt, vt, _, _ = pre(x.reshape(1, b * t, d))
        h1, u, q = (a.reshape(b, t, a.shape[-1]) for a in (h1, u, q))
        kt, vt = (a.reshape(SB_WIDTH, b, t).transpose(1, 0, 2) for a in (kt, vt))
        past = k_hist.shape[1]
        t_pad = -(-t // (n_sub * TQ)) * (n_sub * TQ)
        width = min(past, 4 * TQ)
        assert width >= t_pad
        time_minor = lambda a: a.transpose(0, 2, 3, 1).reshape(b, SB_WIDTH, past)
        widen = lambda a: jnp.pad(a, ((0, 0), (0, 0), (0, width - t)))
        ktb, vtb = _cached_slabs(time_minor(k_hist), time_minor(v_hist), widen(kt), widen(vt))
        q = jnp.pad(q, ((0, 0), (0, t_pad - t), (0, 0)))
        sb = _attention(q, ktb, vtb, n_sub=n_sub, n_pair=n_pair, n_key_blocks=(past + t_pad) // TQ)[:, :t]
    hist = jnp.zeros((b, HALO_ROWS, CONV_CH), F32)
    if conv_hist is not None:
        hist = hist.at[:, HALO_ROWS - (CONV_WIDTH - 1):, :].set(conv_hist)
    y = _post(h1, u, hist, sb, p, w["cw8"], w["cb"], w["lg"], w["lb"],
              w["wo"], w["g2"], w["wgu2"], w["wd2"], w["gp"], w["wpg"], w["wp"], w["gf"], tm=tm_conv, fc=fc)
    assert t >= CONV_WIDTH - 1
    new_conv = u[:, t - (CONV_WIDTH - 1):, :]
    heads = lambda a: a.reshape(b, SB_HEADS, HEAD_DIM, t).transpose(0, 3, 1, 2)[None]
    return y, heads(kt), heads(vt), new_conv[None]


def kernel(x_prompt, x_sample, p_prompt, p_sample, cache_k, cache_v, state_conv, ffn1_norm, ffn1_w_gu, ffn1_w_down, mix_norm, w_in, conv_w, conv_b, conv_ln_g, conv_ln_b, w_out, ffn2_norm, ffn2_w_gu, ffn2_w_down, ple_norm, ple_gate_w, ple_w, final_norm):
    assert ffn1_norm.shape[0] == 1, "single-layer stack"
    row = lambda a: a.reshape(1, -1)
    w = dict(
        g1=ffn1_norm, wgu1=ffn1_w_gu[0].astype(BF16), wd1=ffn1_w_down[0].astype(BF16),
        gm=mix_norm, win=w_in[0][:, :2 * CONV_CH + SB_WIDTH].astype(BF16),
        wkvt=w_in[0][:, 2 * CONV_CH + SB_WIDTH:].T.astype(BF16),
        cw8=jnp.broadcast_to(conv_w[0][:, None, :], (CONV_WIDTH, SUBLANES, CONV_CH)),
        cb=conv_b, lg=conv_ln_g, lb=conv_ln_b, wo=w_out[0].astype(BF16),
        g2=ffn2_norm, wgu2=ffn2_w_gu[0].astype(BF16), wd2=ffn2_w_down[0].astype(BF16),
        gp=ple_norm, wpg=ple_gate_w[0].astype(BF16), wp=ple_w[0].astype(BF16), gf=row(final_norm))
    y_p, k_p, v_p, c_p = _layer(x_prompt, p_prompt[0], None, None, None, w,
                                tm=512, tm_conv=512, n_sub=8, n_pair=1, fc=256)
    y_s, k_s, v_s, c_s = _layer(x_sample, p_sample[0], cache_k[0], cache_v[0], state_conv[0], w,
                                tm=512, tm_conv=64, n_sub=1, n_pair=4, fc=256)
    return y_p, y_s, k_p, v_p, c_p, k_s, v_s, c_s
```

```python
import functools

import jax
import jax.numpy as jnp
from jax import lax
from jax.experimental import pallas as pl
from jax.experimental.pallas import tpu as pltpu

EPS = 1e-6
HEAD_DIM = 64
SB_HEADS = 8
SB_WIDTH = SB_HEADS * HEAD_DIM
CONV_CH = 512
CONV_WIDTH = 31
LOG2_E = 1.4426950408889634
Q_SCALE = LOG2_E / 8.0

LANES = 128
HEADS_PER_STEP = LANES // HEAD_DIM
TQ = LANES
HALO_ROWS = 32
SUBLANES = 8
CONV_ROWS = 32
SHIFT_ROWS = 64
VMEM_LIMIT_BYTES = 60 * 1024 * 1024

DECAY_DONE = 130.0
DEAD = 1e30

F32 = jnp.float32
BF16 = jnp.bfloat16
NT_DIMS = (((1,), (1,)), ((), ()))


def _rms(x, g):
    return x * lax.rsqrt(jnp.mean(x * x, axis=-1, keepdims=True) + EPS) * g


def _dot(a, b):
    return jnp.dot(a, b, preferred_element_type=F32)


def _zero_dependent_on(x):
    bits = lax.bitcast_convert_type(x, jnp.uint32)
    return lax.bitcast_convert_type((bits >> 16) >> 16, F32)


def _add_to_corner(x, piece):
    head = jnp.concatenate([x[0:SUBLANES, 0:LANES] + piece, x[0:SUBLANES, LANES:]], axis=1)
    return jnp.concatenate([head, x[SUBLANES:, :]], axis=0)


def _swiglu(n, wgu_ref, wd_ref, fc, side=()):
    d_ff = wd_ref.shape[0]
    n_chunks = d_ff // fc
    acc = None
    token = None
    for c in range(n_chunks):
        g = _dot(n, wgu_ref[:, c * fc:(c + 1) * fc])
        u = _dot(n, wgu_ref[:, d_ff + c * fc:d_ff + (c + 1) * fc])
        a = g * jax.nn.sigmoid(g) * u
        if token is not None:
            a = _add_to_corner(a, token)
            token = None
        d = _dot(a.astype(BF16), wd_ref[c * fc:(c + 1) * fc, :])
        acc = d if acc is None else acc + d
        for thunk in side[c * len(side) // n_chunks:(c + 1) * len(side) // n_chunks]:
            z = _zero_dependent_on(thunk())
            token = z if token is None else token + z
    return acc if token is None else _add_to_corner(acc, token)


def _pre_kernel(x_ref, g1_ref, wgu_ref, wd_ref, gm_ref, win_ref, wkvt_ref,
                h1_ref, u_ref, q_ref, kt_ref, vt_ref, ktb_ref, vtb_ref, *, fc):
    x = x_ref[...]
    n1 = _rms(x, g1_ref[...]).astype(BF16)
    h1 = x + 0.5 * _swiglu(n1, wgu_ref, wd_ref, fc)
    h1_ref[...] = h1
    n2 = _rms(h1, gm_ref[...]).astype(BF16)
    c = CONV_CH
    a1 = _dot(n2, win_ref[:, 0:c])
    a2 = _dot(n2, win_ref[:, c:2 * c])
    u_ref[...] = a1 * jax.nn.sigmoid(a2)
    q_ref[...] = (_dot(n2, win_ref[:, 2 * c:2 * c + SB_WIDTH]) * Q_SCALE).astype(BF16)
    kt = lax.dot_general(wkvt_ref[0:SB_WIDTH, :], n2, NT_DIMS, preferred_element_type=F32)
    vt = lax.dot_general(wkvt_ref[SB_WIDTH:, :], n2, NT_DIMS, preferred_element_type=F32)
    kt_ref[...] = kt
    vt_ref[...] = vt
    for jj in range(ktb_ref.shape[0]):
        ktb_ref[jj] = kt[:, jj * TQ:(jj + 1) * TQ].astype(BF16)
        vtb_ref[jj] = vt[:, jj * TQ:(jj + 1) * TQ].astype(BF16)


def _resident(shape):
    zeros = (0,) * len(shape)
    return pl.BlockSpec(shape, lambda *_: zeros, pipeline_mode=pl.Buffered(1))


def _pre(x, *, g1, wgu, wd, gm, win, wkvt, tm, fc):
    b, t, d = x.shape
    assert tm % TQ == 0 and t % tm == 0
    tile = lambda w: pl.BlockSpec((None, tm, w), lambda bi, j: (bi, j, 0))
    tmajor = pl.BlockSpec((None, SB_WIDTH, tm), lambda bi, j: (bi, 0, j))
    slabs = pl.BlockSpec((None, tm // TQ, SB_WIDTH, TQ), lambda bi, j: (bi, j, 0, 0))
    out_specs = [tile(d), tile(CONV_CH), tile(SB_WIDTH), tmajor, tmajor, slabs, slabs]
    out_shape = [jax.ShapeDtypeStruct((b, t, d), F32), jax.ShapeDtypeStruct((b, t, CONV_CH), F32),
                 jax.ShapeDtypeStruct((b, t, SB_WIDTH), BF16),
                 jax.ShapeDtypeStruct((b, SB_WIDTH, t), F32), jax.ShapeDtypeStruct((b, SB_WIDTH, t), F32),
                 jax.ShapeDtypeStruct((b, t // TQ, SB_WIDTH, TQ), BF16),
                 jax.ShapeDtypeStruct((b, t // TQ, SB_WIDTH, TQ), BF16)]
    weights = (g1, wgu, wd, gm, win, wkvt)
    return pl.pallas_call(
        functools.partial(_pre_kernel, fc=fc),
        grid=(b, t // tm),
        in_specs=[tile(d)] + [_resident(w.shape) for w in weights],
        out_specs=out_specs,
        out_shape=out_shape,
        compiler_params=pltpu.CompilerParams(
            dimension_semantics=("arbitrary", "arbitrary"), vmem_limit_bytes=VMEM_LIMIT_BYTES),
        name="pre",
    )(x, *weights)


def _after_and_ones(width):
    r = lax.broadcasted_iota(jnp.int32, (width, width + LANES), 0)
    c = lax.broadcasted_iota(jnp.int32, (width, width + LANES), 1)
    return jnp.logical_or(r > c, c >= width).astype(BF16)


def _minus_abs(x):
    bits = lax.bitcast_convert_type(x, jnp.uint32) | jnp.uint32(0x80000000)
    return lax.bitcast_convert_type(bits, F32)


def _attn_kernel(q_ref, kt_ref, vt_ref, o_ref, acc_ref, run_ref, *, n_sub, n_pair, first_block):
    i = pl.program_id(2)
    tq = TQ
    head_of_lane = lax.broadcasted_iota(jnp.int32, (tq, LANES), 1) // HEAD_DIM
    r = lax.broadcasted_iota(jnp.int32, (tq, tq), 0)
    c = lax.broadcasted_iota(jnp.int32, (tq, tq), 1)
    causal = c < r
    chains = [(s, pr, h) for s in range(n_sub) for pr in range(n_pair) for h in range(HEADS_PER_STEP)]

    acc_ref[...] = jnp.zeros_like(acc_ref)

    def q_of(s, pr, h):
        q2 = q_ref[s * tq:(s + 1) * tq, pr * LANES:(pr + 1) * LANES]
        return jnp.where(head_of_lane == h, q2, jnp.zeros_like(q2))

    def slabs(ref, pr, blocks):
        parts = [ref[jb, pr * LANES:(pr + 1) * LANES, :] for jb in blocks]
        return parts[0] if len(parts) == 1 else jnp.concatenate(parts, axis=1)

    def run_round(blocks_of, first_round):
        width = tq * len(blocks_of[0])
        kv = {(s, pr): (slabs(kt_ref, pr, blocks_of[s]), slabs(vt_ref, pr, blocks_of[s]))
              for s in range(n_sub) for pr in range(n_pair)}
        zb = [_dot(q_of(s, pr, h), kv[s, pr][0]) for s, pr, h in chains]
        sp = [jnp.maximum(z, 0.0) + jnp.log2(1.0 + jnp.exp2(_minus_abs(z))) for z in zb]
        if first_round:
            sp = [jnp.concatenate([x[:, :tq], jnp.where(causal, x[:, tq:], 0.0)], axis=1) for x in sp]
        hi = [x.astype(BF16) for x in sp]
        lo = [(x - y.astype(F32)).astype(BF16) for x, y in zip(sp, hi)]
        if first_round:
            after = _after_and_ones(width)[:, :width]
            later = [_dot(x, after) + _dot(y, after) for x, y in zip(hi, lo)]
            total = [jnp.broadcast_to(jnp.sum(x, axis=1, keepdims=True), (tq, LANES)) for x in sp]
        else:
            ao = _after_and_ones(width)
            ao2 = jnp.concatenate([ao, ao], axis=0)
            sums = [_dot(jnp.concatenate([x, y], axis=1), ao2) for x, y in zip(hi, lo)]
            later = [sm[:, :width] for sm in sums]
            total = [sm[:, width:] for sm in sums]
        e = [z - x - y for z, x, y in zip(zb, sp, later)]
        if first_round:
            dead_left = [jnp.where(first_block + i * n_sub + s == 0, DEAD, 0.0) for s, _, _ in chains]
            w = [jnp.concatenate([jnp.exp2(x[:, :tq] - d), jnp.where(causal, jnp.exp2(x[:, tq:]), 0.0)], axis=1)
                 for x, d in zip(e, dead_left)]
            run = total
        else:
            old = [run_ref[ch] for ch in chains]
            w = [jnp.exp2(x - o) for x, o in zip(e, old)]
            run = [o + t for o, t in zip(old, total)]
        for ch, x in zip(chains, w):
            acc_ref[ch] += lax.dot_general(x.astype(BF16), kv[ch[0], ch[1]][1], NT_DIMS,
                                           preferred_element_type=F32)
        for ch, x in zip(chains, run):
            run_ref[ch] = jnp.where(blocks_of[ch[0]][0] <= 0, DEAD, x)

    diag = [first_block + i * n_sub + s for s in range(n_sub)]
    run_round([[jnp.maximum(jd - 1, 0), jd] for jd in diag], True)

    def least_decay():
        return jnp.min(run_ref[...])

    def body(carry):
        rnd, _ = carry
        run_round([[jnp.maximum(jd - 1 - rnd, 0)] for jd in diag], False)
        return rnd + 1, least_decay()

    lax.while_loop(lambda carry: carry[1] < DECAY_DONE, body, (jnp.int32(1), least_decay()))

    for s in range(n_sub):
        for pr in range(n_pair):
            out = jnp.where(head_of_lane == 0, acc_ref[s, pr, 0], acc_ref[s, pr, 1])
            o_ref[s * tq:(s + 1) * tq, pr * LANES:(pr + 1) * LANES] = out.astype(o_ref.dtype)


def _attention(q, kt, vt, *, n_sub, n_pair, n_key_blocks=None):
    b, t_q, _ = q.shape
    n_blocks = kt.shape[1]
    n_key_blocks = n_blocks if n_key_blocks is None else n_key_blocks
    tile = n_sub * TQ
    assert t_q % tile == 0 and (SB_WIDTH // LANES) % n_pair == 0 and t_q // TQ <= n_key_blocks <= n_blocks
    assert kt.shape == vt.shape == (b, n_blocks, SB_WIDTH, TQ)
    wcol = n_pair * LANES
    slab_spec = pl.BlockSpec((None, n_blocks, wcol, TQ), lambda bi, hp, i: (bi, 0, hp, 0))
    return pl.pallas_call(
        functools.partial(_attn_kernel, n_sub=n_sub, n_pair=n_pair, first_block=n_key_blocks - t_q // TQ),
        grid=(b, SB_WIDTH // wcol, t_q // tile),
        in_specs=[pl.BlockSpec((None, tile, wcol), lambda bi, hp, i: (bi, i, hp)), slab_spec, slab_spec],
        out_specs=pl.BlockSpec((None, tile, wcol), lambda bi, hp, i: (bi, i, hp)),
        out_shape=jax.ShapeDtypeStruct((b, t_q, SB_WIDTH), BF16),
        scratch_shapes=[pltpu.VMEM((n_sub, n_pair, HEADS_PER_STEP, TQ, LANES), F32),
                        pltpu.VMEM((n_sub, n_pair, HEADS_PER_STEP, TQ, LANES), F32)],
        compiler_params=pltpu.CompilerParams(
            dimension_semantics=("arbitrary", "arbitrary", "arbitrary"),
            vmem_limit_bytes=VMEM_LIMIT_BYTES),
        name="attn",
    )(q, kt, vt)


def _slab_kernel(kc_ref, vc_ref, kn_ref, vn_ref, ko_ref, vo_ref):
    is_cache = pl.program_id(1) < pl.num_programs(1) - 1
    for cache_ref, new_ref, out_ref in ((kc_ref, kn_ref, ko_ref), (vc_ref, vn_ref, vo_ref)):
        x = jnp.where(is_cache, cache_ref[...], new_ref[...])
        for jj in range(out_ref.shape[0]):
            out_ref[jj] = x[:, jj * TQ:(jj + 1) * TQ].astype(BF16)


def _cached_slabs(kt_cache, vt_cache, kt_new, vt_new):
    b, feat, past = kt_cache.shape
    width = kt_new.shape[2]
    assert past % width == 0 and width % TQ == 0
    steps = past // width
    cache_spec = pl.BlockSpec((None, feat, width), lambda bi, j: (bi, 0, jnp.minimum(j, steps - 1)))
    new_spec = pl.BlockSpec((None, feat, width), lambda bi, j: (bi, 0, 0))
    out_spec = pl.BlockSpec((None, width // TQ, feat, TQ), lambda bi, j: (bi, j, 0, 0))
    out_shape = jax.ShapeDtypeStruct((b, (past + width) // TQ, feat, TQ), BF16)
    return pl.pallas_call(
        _slab_kernel,
        grid=(b, steps + 1),
        in_specs=[cache_spec, cache_spec, new_spec, new_spec],
        out_specs=[out_spec, out_spec],
        out_shape=[out_shape, out_shape],
        compiler_params=pltpu.CompilerParams(
            dimension_semantics=("arbitrary", "arbitrary"), vmem_limit_bytes=VMEM_LIMIT_BYTES),
        name="slabs",
    )(kt_cache, vt_cache, kt_new, vt_new)


def _conv_module(u_ref, left, out_ref, sh_ref, cw_ref, cb_ref, lg_ref, lb_ref, n_seq):
    seq = u_ref.shape[0] // n_seq
    span = seq + HALO_ROWS
    pad = HALO_ROWS - (CONV_WIDTH - 1)
    for s in range(n_seq):
        sh_ref[0, s * span:s * span + HALO_ROWS, :] = left[s * HALO_ROWS:(s + 1) * HALO_ROWS]
        sh_ref[0, s * span + HALO_ROWS:(s + 1) * span, :] = u_ref[s * seq:(s + 1) * seq, :]
    total = n_seq * span - SUBLANES

    def shift(c0):
        rows = min(SHIFT_ROWS, total - c0)
        chunk = sh_ref[0, c0:c0 + rows + SUBLANES, :]
        for r in range(1, SUBLANES):
            rolled = pltpu.roll(chunk, rows + SUBLANES - r, axis=0)[0:rows]
            sh_ref[r, c0:c0 + rows, :] = rolled
        return rolled[0:SUBLANES, 0:LANES]

    def rows(s, r0):
        y = cb_ref[...]
        for t in range(CONV_WIDTH):
            r, a = (pad + t) % SUBLANES, (pad + t) // SUBLANES
            w_t = jnp.concatenate([cw_ref[t]] * (CONV_ROWS // SUBLANES), axis=0)
            start = s * span + r0 + SUBLANES * a
            y = y + w_t * sh_ref[r, start:start + CONV_ROWS, :]
        yc = y - jnp.mean(y, axis=-1, keepdims=True)
        yn = yc * lax.rsqrt(jnp.mean(yc * yc, axis=-1, keepdims=True) + EPS) * lg_ref[...] + lb_ref[...]
        out = yn * jax.nn.sigmoid(yn)
        out_ref[s * seq + r0:s * seq + r0 + CONV_ROWS, :] = out.astype(BF16)
        return out[0:SUBLANES, 0:LANES]

    return ([functools.partial(shift, c0) for c0 in range(0, total, SHIFT_ROWS)],
            [functools.partial(rows, s, r0) for s in range(n_seq) for r0 in range(0, seq, CONV_ROWS)])


def _post_kernel(h1_ref, u0_ref, hist0_ref, un_ref, halon_ref, histn_ref, sb_ref, p_ref,
                 cw_ref, cb_ref, lg_ref, lb_ref, wo_ref,
                 g2_ref, wgu_ref, wd_ref, gp_ref, wpg_ref, wp_ref, gf_ref,
                 y_ref, sh_ref, conv_ref, *, fc, n_seq):
    conv = functools.partial(_conv_module, out_ref=conv_ref, sh_ref=sh_ref, cw_ref=cw_ref, cb_ref=cb_ref,
                             lg_ref=lg_ref, lb_ref=lb_ref, n_seq=n_seq)
    j = pl.program_id(1)

    @pl.when(jnp.logical_and(pl.program_id(0) == 0, j == 0))
    def _():
        shifts, row_groups = conv(u0_ref, hist0_ref[...])
        for thunk in shifts + row_groups:
            thunk()

    opens_sequence = j == pl.num_programs(1) - 1
    left = histn_ref[...] if n_seq > 1 else jnp.where(opens_sequence, histn_ref[...], halon_ref[...])
    shifts, row_groups = conv(un_ref, left)
    token = None
    for thunk in shifts:
        z = _zero_dependent_on(thunk())
        token = z if token is None else token + z
    mixed = _dot(conv_ref[...], wo_ref[0:CONV_CH, :]) + _dot(sb_ref[...], wo_ref[CONV_CH:, :])
    h2 = h1_ref[...] + _add_to_corner(mixed, token)
    n2 = _rms(h2, g2_ref[...]).astype(BF16)
    h3 = h2 + 0.5 * _swiglu(n2, wgu_ref, wd_ref, fc, side=row_groups)
    n3 = _rms(h3, gp_ref[...]).astype(BF16)
    gate = jax.nn.sigmoid(_dot(n3, wpg_ref[...]))
    h4 = h3 + gate * _dot(p_ref[...].astype(BF16), wp_ref[...])
    y_ref[...] = _rms(h4, gf_ref[...])


def _post(h1, u, hist, sb, p, *, cw, cb, lg, lb, wo, g2, wgu, wd, gp, wpg, wp, gf, tm, fc, n_seq=1):
    b, t, d = h1.shape
    nj = t // tm
    assert t % tm == 0 and (tm // n_seq) % CONV_ROWS == 0 and tm % HALO_ROWS == 0 and (n_seq == 1 or nj == 1)
    halo_blocks = tm // HALO_ROWS

    def nxt(bi, j):
        wrap = (j + 1 == nj).astype(jnp.int32)
        return jnp.minimum(bi + wrap, b - 1), (j + 1) * (1 - wrap)

    def halo_index(bi, j):
        bn, jn = nxt(bi, j)
        return bn, jnp.maximum(jn * halo_blocks - 1, 0), 0

    tile = lambda w: pl.BlockSpec((None, tm, w), lambda bi, j: (bi, j, 0))
    once = lambda r: pl.BlockSpec((None, r, CONV_CH), lambda bi, j: (0, 0, 0), pipeline_mode=pl.Buffered(1))
    hist_rows = n_seq * HALO_ROWS
    weights = (cw, cb, lg, lb, wo, g2, wgu, wd, gp, wpg, wp, gf)
    return pl.pallas_call(
        functools.partial(_post_kernel, fc=fc, n_seq=n_seq),
        grid=(b, nj),
        in_specs=[tile(d), once(tm), once(hist_rows),
                  pl.BlockSpec((None, tm, CONV_CH), lambda bi, j: (*nxt(bi, j), 0)),
                  pl.BlockSpec((None, HALO_ROWS, CONV_CH), halo_index),
                  pl.BlockSpec((None, hist_rows, CONV_CH), lambda bi, j: (nxt(bi, j)[0], 0, 0)),
                  tile(SB_WIDTH), tile(p.shape[-1])]
                 + [_resident(w.shape) for w in weights],
        out_specs=tile(d),
        out_shape=jax.ShapeDtypeStruct((b, t, d), F32),
        scratch_shapes=[pltpu.VMEM((SUBLANES, hist_rows + tm, CONV_CH), F32),
                        pltpu.VMEM((tm, CONV_CH), BF16)],
        compiler_params=pltpu.CompilerParams(
            dimension_semantics=("arbitrary", "arbitrary"), vmem_limit_bytes=VMEM_LIMIT_BYTES),
        name="post",
    )(h1, u, hist, u, u, hist, sb, p, *weights)


def _layer(x, p, k_hist, v_hist, conv_hist, w, *, tm, tm_conv, n_sub, n_pair, fc):
    b, t, d = x.shape
    pre = functools.partial(_pre, g1=w["g1"], wgu=w["wgu1"], wd=w["wd1"], gm=w["gm"], win=w["win"],
                            wkvt=w["wkvt"], tm=tm, fc=fc)
    if k_hist is None:
        h1, u, q, kt, vt, ktb, vtb = pre(x)
        sb = _attention(q, ktb, vtb, n_sub=n_sub, n_pair=n_pair)
    else:
        h1, u, q, kt, vt, _, _ = pre(x.reshape(1, b * t, d))
        h1, u, q = (a.reshape(b, t, a.shape[-1]) for a in (h1, u, q))
        kt, vt = (a.reshape(SB_WIDTH, b, t).transpose(1, 0, 2) for a in (kt, vt))
        past = k_hist.shape[1]
        t_pad = -(-t // (n_sub * TQ)) * (n_sub * TQ)
        width = min(past, 4 * TQ)
        assert width >= t_pad
        time_minor = lambda a: a.transpose(0, 2, 3, 1).reshape(b, SB_WIDTH, past)
        widen = lambda a: jnp.pad(a, ((0, 0), (0, 0), (0, width - t)))
        ktb, vtb = _cached_slabs(time_minor(k_hist), time_minor(v_hist), widen(kt), widen(vt))
        q = jnp.pad(q, ((0, 0), (0, t_pad - t), (0, 0)))
        sb = _attention(q, ktb, vtb, n_sub=n_sub, n_pair=n_pair, n_key_blocks=(past + t_pad) // TQ)[:, :t]
    hist = jnp.zeros((b, HALO_ROWS, CONV_CH), F32)
    if conv_hist is not None:
        hist = hist.at[:, HALO_ROWS - (CONV_WIDTH - 1):, :].set(conv_hist)
    post = functools.partial(_post, cw=w["cw8"], cb=w["cb"], lg=w["lg"], lb=w["lb"], wo=w["wo"], g2=w["g2"],
                             wgu=w["wgu2"], wd=w["wd2"], gp=w["gp"], wpg=w["wpg"], wp=w["wp"], gf=w["gf"], fc=fc)
    if k_hist is None:
        y = post(h1, u, hist, sb, p, tm=tm_conv)
    else:
        one = lambda a: a.reshape(1, b * a.shape[1], a.shape[2])
        y = post(one(h1), one(u), one(hist), one(sb), one(p), tm=b * t, n_seq=b).reshape(b, t, d)
    assert t >= CONV_WIDTH - 1
    new_conv = u[:, t - (CONV_WIDTH - 1):, :]
    heads = lambda a: a.reshape(b, SB_HEADS, HEAD_DIM, t).transpose(0, 3, 1, 2)[None]
    return y, heads(kt), heads(vt), new_conv[None]


def kernel(x_prompt, x_sample, p_prompt, p_sample, cache_k, cache_v, state_conv, ffn1_norm, ffn1_w_gu, ffn1_w_down, mix_norm, w_in, conv_w, conv_b, conv_ln_g, conv_ln_b, w_out, ffn2_norm, ffn2_w_gu, ffn2_w_down, ple_norm, ple_gate_w, ple_w, final_norm):
    assert ffn1_norm.shape[0] == 1, "single-layer stack"
    row = lambda a: a.reshape(1, -1)
    w = dict(
        g1=ffn1_norm, wgu1=ffn1_w_gu[0].astype(BF16), wd1=ffn1_w_down[0].astype(BF16),
        gm=mix_norm, win=w_in[0][:, :2 * CONV_CH + SB_WIDTH].astype(BF16),
        wkvt=w_in[0][:, 2 * CONV_CH + SB_WIDTH:].T.astype(BF16),
        cw8=jnp.broadcast_to(conv_w[0][:, None, :], (CONV_WIDTH, SUBLANES, CONV_CH)),
        cb=conv_b, lg=conv_ln_g, lb=conv_ln_b, wo=w_out[0].astype(BF16),
        g2=ffn2_norm, wgu2=ffn2_w_gu[0].astype(BF16), wd2=ffn2_w_down[0].astype(BF16),
        gp=ple_norm, wpg=ple_gate_w[0].astype(BF16), wp=ple_w[0].astype(BF16), gf=row(final_norm))
    y_p, k_p, v_p, c_p = _layer(x_prompt, p_prompt[0], None, None, None, w,
                                tm=512, tm_conv=512, n_sub=8, n_pair=1, fc=256)
    y_s, k_s, v_s, c_s = _layer(x_sample, p_sample[0], cache_k[0], cache_v[0], state_conv[0], w,
                                tm=512, tm_conv=64, n_sub=1, n_pair=4, fc=256)
    return y_p, y_s, k_p, v_p, c_p, k_s, v_s, c_s
```

```python
import functools

import jax
import jax.numpy as jnp
from jax import lax
from jax.experimental import pallas as pl
from jax.experimental.pallas import tpu as pltpu

EPS = 1e-6
HEAD_DIM = 64
SB_HEADS = 8
SB_WIDTH = SB_HEADS * HEAD_DIM
CONV_CH = 512
CONV_WIDTH = 31
LOG2_E = 1.4426950408889634
Q_SCALE = LOG2_E / 8.0

LANES = 128
HEADS_PER_STEP = LANES // HEAD_DIM
TQ = LANES
HALO_ROWS = 32
SUBLANES = 8
CONV_ROWS = 32
SHIFT_ROWS = 64
VMEM_LIMIT_BYTES = 60 * 1024 * 1024

DECAY_DONE = 130.0
DEAD = 1e30

F32 = jnp.float32
BF16 = jnp.bfloat16
NT_DIMS = (((1,), (1,)), ((), ()))


def _rms(x, g):
    return x * lax.rsqrt(jnp.mean(x * x, axis=-1, keepdims=True) + EPS) * g


def _dot(a, b):
    return jnp.dot(a, b, preferred_element_type=F32)


def _zero_dependent_on(x):
    bits = lax.bitcast_convert_type(x, jnp.uint32)
    return lax.bitcast_convert_type((bits >> 16) >> 16, F32)


def _add_to_corner(x, piece):
    head = jnp.concatenate([x[0:SUBLANES, 0:LANES] + piece, x[0:SUBLANES, LANES:]], axis=1)
    return jnp.concatenate([head, x[SUBLANES:, :]], axis=0)


def _swiglu(n, wgu_ref, wd_ref, fc, side=()):
    d_ff = wd_ref.shape[0]
    n_chunks = d_ff // fc
    acc = None
    token = None
    for c in range(n_chunks):
        g = _dot(n, wgu_ref[:, c * fc:(c + 1) * fc])
        u = _dot(n, wgu_ref[:, d_ff + c * fc:d_ff + (c + 1) * fc])
        a = g * jax.nn.sigmoid(g) * u
        if token is not None:
            a = _add_to_corner(a, token)
            token = None
        d = _dot(a.astype(BF16), wd_ref[c * fc:(c + 1) * fc, :])
        acc = d if acc is None else acc + d
        for thunk in side[c * len(side) // n_chunks:(c + 1) * len(side) // n_chunks]:
            z = _zero_dependent_on(thunk())
            token = z if token is None else token + z
    return acc if token is None else _add_to_corner(acc, token)


def _pre_kernel(x_ref, g1_ref, wgu_ref, wd_ref, gm_ref, win_ref, wkvt_ref,
                h1_ref, u_ref, q_ref, kt_ref, vt_ref, ktb_ref, vtb_ref, *, fc):
    x = x_ref[...]
    n1 = _rms(x, g1_ref[...]).astype(BF16)
    h1 = x + 0.5 * _swiglu(n1, wgu_ref, wd_ref, fc)
    h1_ref[...] = h1
    n2 = _rms(h1, gm_ref[...]).astype(BF16)
    c = CONV_CH
    a1 = _dot(n2, win_ref[:, 0:c])
    a2 = _dot(n2, win_ref[:, c:2 * c])
    u_ref[...] = a1 * jax.nn.sigmoid(a2)
    q_ref[...] = (_dot(n2, win_ref[:, 2 * c:2 * c + SB_WIDTH]) * Q_SCALE).astype(BF16)
    kt = lax.dot_general(wkvt_ref[0:SB_WIDTH, :], n2, NT_DIMS, preferred_element_type=F32)
    vt = lax.dot_general(wkvt_ref[SB_WIDTH:, :], n2, NT_DIMS, preferred_element_type=F32)
    kt_ref[...] = kt
    vt_ref[...] = vt
    for jj in range(ktb_ref.shape[0]):
        ktb_ref[jj] = kt[:, jj * TQ:(jj + 1) * TQ].astype(BF16)
        vtb_ref[jj] = vt[:, jj * TQ:(jj + 1) * TQ].astype(BF16)


def _resident(shape):
    zeros = (0,) * len(shape)
    return pl.BlockSpec(shape, lambda *_: zeros, pipeline_mode=pl.Buffered(1))


def _pre(x, *, g1, wgu, wd, gm, win, wkvt, tm, fc):
    b, t, d = x.shape
    assert tm % TQ == 0 and t % tm == 0
    tile = lambda w: pl.BlockSpec((None, tm, w), lambda bi, j: (bi, j, 0))
    tmajor = pl.BlockSpec((None, SB_WIDTH, tm), lambda bi, j: (bi, 0, j))
    slabs = pl.BlockSpec((None, tm // TQ, SB_WIDTH, TQ), lambda bi, j: (bi, j, 0, 0))
    out_specs = [tile(d), tile(CONV_CH), tile(SB_WIDTH), tmajor, tmajor, slabs, slabs]
    out_shape = [jax.ShapeDtypeStruct((b, t, d), F32), jax.ShapeDtypeStruct((b, t, CONV_CH), F32),
                 jax.ShapeDtypeStruct((b, t, SB_WIDTH), BF16),
                 jax.ShapeDtypeStruct((b, SB_WIDTH, t), F32), jax.ShapeDtypeStruct((b, SB_WIDTH, t), F32),
                 jax.ShapeDtypeStruct((b, t // TQ, SB_WIDTH, TQ), BF16),
                 jax.ShapeDtypeStruct((b, t // TQ, SB_WIDTH, TQ), BF16)]
    weights = (g1, wgu, wd, gm, win, wkvt)
    return pl.pallas_call(
        functools.partial(_pre_kernel, fc=fc),
        grid=(b, t // tm),
        in_specs=[tile(d)] + [_resident(w.shape) for w in weights],
        out_specs=out_specs,
        out_shape=out_shape,
        compiler_params=pltpu.CompilerParams(
            dimension_semantics=("arbitrary", "arbitrary"), vmem_limit_bytes=VMEM_LIMIT_BYTES),
        name="pre",
    )(x, *weights)


def _after_and_ones(width):
    r = lax.broadcasted_iota(jnp.int32, (width, width + LANES), 0)
    c = lax.broadcasted_iota(jnp.int32, (width, width + LANES), 1)
    return jnp.logical_or(r > c, c >= width).astype(BF16)


def _minus_abs(x):
    bits = lax.bitcast_convert_type(x, jnp.uint32) | jnp.uint32(0x80000000)
    return lax.bitcast_convert_type(bits, F32)


def _attn_kernel(q_ref, kt_ref, vt_ref, o_ref, acc_ref, run_ref, *, n_sub, n_pair, first_block):
    i = pl.program_id(2)
    tq = TQ
    head_of_lane = lax.broadcasted_iota(jnp.int32, (tq, LANES), 1) // HEAD_DIM
    r = lax.broadcasted_iota(jnp.int32, (tq, tq), 0)
    c = lax.broadcasted_iota(jnp.int32, (tq, tq), 1)
    causal = c < r
    chains = [(s, pr, h) for s in range(n_sub) for pr in range(n_pair) for h in range(HEADS_PER_STEP)]

    acc_ref[...] = jnp.zeros_like(acc_ref)

    def q_of(s, pr, h):
        q2 = q_ref[s * tq:(s + 1) * tq, pr * LANES:(pr + 1) * LANES]
        return jnp.where(head_of_lane == h, q2, jnp.zeros_like(q2))

    def slabs(ref, pr, blocks):
        parts = [ref[jb, pr * LANES:(pr + 1) * LANES, :] for jb in blocks]
        return parts[0] if len(parts) == 1 else jnp.concatenate(parts, axis=1)

    def run_round(blocks_of, first_round):
        width = tq * len(blocks_of[0])
        kv = {(s, pr): (slabs(kt_ref, pr, blocks_of[s]), slabs(vt_ref, pr, blocks_of[s]))
              for s in range(n_sub) for pr in range(n_pair)}
        zb = [_dot(q_of(s, pr, h), kv[s, pr][0]) for s, pr, h in chains]
        sp = [jnp.maximum(z, 0.0) + jnp.log2(1.0 + jnp.exp2(_minus_abs(z))) for z in zb]
        if first_round:
            sp = [jnp.concatenate([x[:, :tq], jnp.where(causal, x[:, tq:], 0.0)], axis=1) for x in sp]
        hi = [x.astype(BF16) for x in sp]
        lo = [(x - y.astype(F32)).astype(BF16) for x, y in zip(sp, hi)]
        if first_round:
            after = _after_and_ones(width)[:, :width]
            later = [_dot(x, after) + _dot(y, after) for x, y in zip(hi, lo)]
            total = [jnp.broadcast_to(jnp.sum(x, axis=1, keepdims=True), (tq, LANES)) for x in sp]
        else:
            ao = _after_and_ones(width)
            ao2 = jnp.concatenate([ao, ao], axis=0)
            sums = [_dot(jnp.concatenate([x, y], axis=1), ao2) for x, y in zip(hi, lo)]
            later = [sm[:, :width] for sm in sums]
            total = [sm[:, width:] for sm in sums]
        e = [z - x - y for z, x, y in zip(zb, sp, later)]
        if first_round:
            dead_left = [jnp.where(first_block + i * n_sub + s == 0, DEAD, 0.0) for s, _, _ in chains]
            w = [jnp.concatenate([jnp.exp2(x[:, :tq] - d), jnp.where(causal, jnp.exp2(x[:, tq:]), 0.0)], axis=1)
                 for x, d in zip(e, dead_left)]
            run = total
        else:
            old = [run_ref[ch] for ch in chains]
            w = [jnp.exp2(x - o) for x, o in zip(e, old)]
            run = [o + t for o, t in zip(old, total)]
        for ch, x in zip(chains, w):
            acc_ref[ch] += lax.dot_general(x.astype(BF16), kv[ch[0], ch[1]][1], NT_DIMS,
                                           preferred_element_type=F32)
        for ch, x in zip(chains, run):
            run_ref[ch] = jnp.where(blocks_of[ch[0]][0] <= 0, DEAD, x)

    diag = [first_block + i * n_sub + s for s in range(n_sub)]
    run_round([[jnp.maximum(jd - 1, 0), jd] for jd in diag], True)

    def least_decay():
        return jnp.min(run_ref[...])

    def body(carry):
        rnd, _ = carry
        run_round([[jnp.maximum(jd - 1 - rnd, 0)] for jd in diag], False)
        return rnd + 1, least_decay()

    lax.while_loop(lambda carry: carry[1] < DECAY_DONE, body, (jnp.int32(1), least_decay()))

    for s in range(n_sub):
        for pr in range(n_pair):
            out = jnp.where(head_of_lane == 0, acc_ref[s, pr, 0], acc_ref[s, pr, 1])
            o_ref[s * tq:(s + 1) * tq, pr * LANES:(pr + 1) * LANES] = out.astype(o_ref.dtype)


def _attention(q, kt, vt, *, n_sub, n_pair, n_key_blocks=None):
    b, t_q, _ = q.shape
    n_blocks = kt.shape[1]
    n_key_blocks = n_blocks if n_key_blocks is None else n_key_blocks
    tile = n_sub * TQ
    assert t_q % tile == 0 and (SB_WIDTH // LANES) % n_pair == 0 and t_q // TQ <= n_key_blocks <= n_blocks
    assert kt.shape == vt.shape == (b, n_blocks, SB_WIDTH, TQ)
    wcol = n_pair * LANES
    slab_spec = pl.BlockSpec((None, n_blocks, wcol, TQ), lambda bi, hp, i: (bi, 0, hp, 0))
    return pl.pallas_call(
        functools.partial(_attn_kernel, n_sub=n_sub, n_pair=n_pair, first_block=n_key_blocks - t_q // TQ),
        grid=(b, SB_WIDTH // wcol, t_q // tile),
        in_specs=[pl.BlockSpec((None, tile, wcol), lambda bi, hp, i: (bi, i, hp)), slab_spec, slab_spec],
        out_specs=pl.BlockSpec((None, tile, wcol), lambda bi, hp, i: (bi, i, hp)),
        out_shape=jax.ShapeDtypeStruct((b, t_q, SB_WIDTH), BF16),
        scratch_shapes=[pltpu.VMEM((n_sub, n_pair, HEADS_PER_STEP, TQ, LANES), F32),
                        pltpu.VMEM((n_sub, n_pair, HEADS_PER_STEP, TQ, LANES), F32)],
        compiler_params=pltpu.CompilerParams(
            dimension_semantics=("arbitrary", "arbitrary", "arbitrary"),
            vmem_limit_bytes=VMEM_LIMIT_BYTES),
        name="attn",
    )(q, kt, vt)


def _slab_kernel(kc_ref, vc_ref, kn_ref, vn_ref, ko_ref, vo_ref):
    is_cache = pl.program_id(1) < pl.num_programs(1) - 1
    for cache_ref, new_ref, out_ref in ((kc_ref, kn_ref, ko_ref), (vc_ref, vn_ref, vo_ref)):
        x = jnp.where(is_cache, cache_ref[...], new_ref[...])
        for jj in range(out_ref.shape[0]):
            out_ref[jj] = x[:, jj * TQ:(jj + 1) * TQ].astype(BF16)


def _cached_slabs(kt_cache, vt_cache, kt_new, vt_new):
    b, feat, past = kt_cache.shape
    width = kt_new.shape[2]
    assert past % width == 0 and width % TQ == 0
    steps = past // width
    cache_spec = pl.BlockSpec((None, feat, width), lambda bi, j: (bi, 0, jnp.minimum(j, steps - 1)))
    new_spec = pl.BlockSpec((None, feat, width), lambda bi, j: (bi, 0, 0))
    out_spec = pl.BlockSpec((None, width // TQ, feat, TQ), lambda bi, j: (bi, j, 0, 0))
    out_shape = jax.ShapeDtypeStruct((b, (past + width) // TQ, feat, TQ), BF16)
    return pl.pallas_call(
        _slab_kernel,
        grid=(b, steps + 1),
        in_specs=[cache_spec, cache_spec, new_spec, new_spec],
        out_specs=[out_spec, out_spec],
        out_shape=[out_shape, out_shape],
        compiler_params=pltpu.CompilerParams(
            dimension_semantics=("arbitrary", "arbitrary"), vmem_limit_bytes=VMEM_LIMIT_BYTES),
        name="slabs",
    )(kt_cache, vt_cache, kt_new, vt_new)


def _conv_module(u_ref, left, out_ref, sh_ref, cw_ref, cb_ref, lg_ref, lb_ref, n_seq):
    seq = u_ref.shape[0] // n_seq
    span = seq + HALO_ROWS
    pad = HALO_ROWS - (CONV_WIDTH - 1)
    for s in range(n_seq):
        sh_ref[0, s * span:s * span + HALO_ROWS, :] = left[s * HALO_ROWS:(s + 1) * HALO_ROWS]
        sh_ref[0, s * span + HALO_ROWS:(s + 1) * span, :] = u_ref[s * seq:(s + 1) * seq, :]
    total = n_seq * span - SUBLANES

    def shift(c0):
        rows = min(SHIFT_ROWS, total - c0)
        chunk = sh_ref[0, c0:c0 + rows + SUBLANES, :]
        for r in range(1, SUBLANES):
            rolled = pltpu.roll(chunk, rows + SUBLANES - r, axis=0)[0:rows]
            sh_ref[r, c0:c0 + rows, :] = rolled
        return rolled[0:SUBLANES, 0:LANES]

    def rows(s, r0):
        y = cb_ref[...]
        for t in range(CONV_WIDTH):
            r, a = (pad + t) % SUBLANES, (pad + t) // SUBLANES
            w_t = jnp.concatenate([cw_ref[t]] * (CONV_ROWS // SUBLANES), axis=0)
            start = s * span + r0 + SUBLANES * a
            y = y + w_t * sh_ref[r, start:start + CONV_ROWS, :]
        yc = y - jnp.mean(y, axis=-1, keepdims=True)
        yn = yc * lax.rsqrt(jnp.mean(yc * yc, axis=-1, keepdims=True) + EPS) * lg_ref[...] + lb_ref[...]
        out = yn * jax.nn.sigmoid(yn)
        out_ref[s * seq + r0:s * seq + r0 + CONV_ROWS, :] = out.astype(BF16)
        return out[0:SUBLANES, 0:LANES]

    return ([functools.partial(shift, c0) for c0 in range(0, total, SHIFT_ROWS)],
            [functools.partial(rows, s, r0) for s in range(n_seq) for r0 in range(0, seq, CONV_ROWS)])


def _post_kernel(h1_ref, u0_ref, hist0_ref, un_ref, halon_ref, histn_ref, sb_ref, p_ref,
                 cw_ref, cb_ref, lg_ref, lb_ref, wo_ref,
                 g2_ref, wgu_ref, wd_ref, gp_ref, wpg_ref, wp_ref, gf_ref,
                 y_ref, sh_ref, conv_ref, *, fc, n_seq):
    conv = functools.partial(_conv_module, out_ref=conv_ref, sh_ref=sh_ref, cw_ref=cw_ref, cb_ref=cb_ref,
                             lg_ref=lg_ref, lb_ref=lb_ref, n_seq=n_seq)
    j = pl.program_id(1)

    @pl.when(jnp.logical_and(pl.program_id(0) == 0, j == 0))
    def _():
        shifts, row_groups = conv(u0_ref, hist0_ref[...])
        for thunk in shifts + row_groups:
            thunk()

    opens_sequence = j == pl.num_programs(1) - 1
    left = histn_ref[...] if n_seq > 1 else jnp.where(opens_sequence, histn_ref[...], halon_ref[...])
    shifts, row_groups = conv(un_ref, left)
    token = None
    for thunk in shifts:
        z = _zero_dependent_on(thunk())
        token = z if token is None else token + z
    mixed = _dot(conv_ref[...], wo_ref[0:CONV_CH, :]) + _dot(sb_ref[...], wo_ref[CONV_CH:, :])
    h2 = h1_ref[...] + _add_to_corner(mixed, token)
    n2 = _rms(h2, g2_ref[...]).astype(BF16)
    h3 = h2 + 0.5 * _swiglu(n2, wgu_ref, wd_ref, fc, side=row_groups)
    n3 = _rms(h3, gp_ref[...]).astype(BF16)
    gate = jax.nn.sigmoid(_dot(n3, wpg_ref[...]))
    h4 = h3 + gate * _dot(p_ref[...].astype(BF16), wp_ref[...])
    y_ref[...] = _rms(h4, gf_ref[...])


def _post(h1, u, hist, sb, p, *, cw, cb, lg, lb, wo, g2, wgu, wd, gp, wpg, wp, gf, tm, fc, n_seq=1):
    b, t, d = h1.shape
    nj = t // tm
    assert t % tm == 0 and (tm // n_seq) % CONV_ROWS == 0 and tm % HALO_ROWS == 0 and (n_seq == 1 or nj == 1)
    halo_blocks = tm // HALO_ROWS

    def nxt(bi, j):
        wrap = (j + 1 == nj).astype(jnp.int32)
        return jnp.minimum(bi + wrap, b - 1), (j + 1) * (1 - wrap)

    def halo_index(bi, j):
        bn, jn = nxt(bi, j)
        return bn, jnp.maximum(jn * halo_blocks - 1, 0), 0

    tile = lambda w: pl.BlockSpec((None, tm, w), lambda bi, j: (bi, j, 0))
    once = lambda r: pl.BlockSpec((None, r, CONV_CH), lambda bi, j: (0, 0, 0), pipeline_mode=pl.Buffered(1))
    hist_rows = n_seq * HALO_ROWS
    weights = (cw, cb, lg, lb, wo, g2, wgu, wd, gp, wpg, wp, gf)
    return pl.pallas_call(
        functools.partial(_post_kernel, fc=fc, n_seq=n_seq),
        grid=(b, nj),
        in_specs=[tile(d), once(tm), once(hist_rows),
                  pl.BlockSpec((None, tm, CONV_CH), lambda bi, j: (*nxt(bi, j), 0)),
                  pl.BlockSpec((None, HALO_ROWS, CONV_CH), halo_index),
                  pl.BlockSpec((None, hist_rows, CONV_CH), lambda bi, j: (nxt(bi, j)[0], 0, 0)),
                  tile(SB_WIDTH), tile(p.shape[-1])]
                 + [_resident(w.shape) for w in weights],
        out_specs=tile(d),
        out_shape=jax.ShapeDtypeStruct((b, t, d), F32),
        scratch_shapes=[pltpu.VMEM((SUBLANES, hist_rows + tm, CONV_CH), F32),
                        pltpu.VMEM((tm, CONV_CH), BF16)],
        compiler_params=pltpu.CompilerParams(
            dimension_semantics=("arbitrary", "arbitrary"), vmem_limit_bytes=VMEM_LIMIT_BYTES),
        name="post",
    )(h1, u, hist, u, u, hist, sb, p, *weights)


def _layer(x, p, k_hist, v_hist, conv_hist, w, *, tm, tm_conv, n_sub, n_pair, fc):
    b, t, d = x.shape
    pre = functools.partial(_pre, g1=w["g1"], wgu=w["wgu1"], wd=w["wd1"], gm=w["gm"], win=w["win"],
                            wkvt=w["wkvt"], tm=tm, fc=fc)
    if k_hist is None:
        h1, u, q, kt, vt, ktb, vtb = pre(x)
        sb = _attention(q, ktb, vtb, n_sub=n_sub, n_pair=n_pair)
    else:
        h1, u, q, kt, vt, _, _ = pre(x.reshape(1, b * t, d))
        h1, u, q = (a.reshape(b, t, a.shape[-1]) for a in (h1, u, q))
        kt, vt = (a.reshape(SB_WIDTH, b, t).transpose(1, 0, 2) for a in (kt, vt))
        past = k_hist.shape[1]
        t_pad = -(-t // (n_sub * TQ)) * (n_sub * TQ)
        width = min(past, 8 * TQ)
        assert width >= t_pad
        time_minor = lambda a: a.transpose(0, 2, 3, 1).reshape(b, SB_WIDTH, past)
        widen = lambda a: jnp.pad(a, ((0, 0), (0, 0), (0, width - t)))
        ktb, vtb = _cached_slabs(time_minor(k_hist), time_minor(v_hist), widen(kt), widen(vt))
        q = jnp.pad(q, ((0, 0), (0, t_pad - t), (0, 0)))
        sb = _attention(q, ktb, vtb, n_sub=n_sub, n_pair=n_pair, n_key_blocks=(past + t_pad) // TQ)[:, :t]
    hist = jnp.zeros((b, HALO_ROWS, CONV_CH), F32)
    if conv_hist is not None:
        hist = hist.at[:, HALO_ROWS - (CONV_WIDTH - 1):, :].set(conv_hist)
    post = functools.partial(_post, cw=w["cw8"], cb=w["cb"], lg=w["lg"], lb=w["lb"], wo=w["wo"], g2=w["g2"],
                             wgu=w["wgu2"], wd=w["wd2"], gp=w["gp"], wpg=w["wpg"], wp=w["wp"], gf=w["gf"], fc=fc)
    if k_hist is None:
        y = post(h1, u, hist, sb, p, tm=tm_conv)
    else:
        one = lambda a: a.reshape(1, b * a.shape[1], a.shape[2])
        y = post(one(h1), one(u), one(hist), one(sb), one(p), tm=b * t, n_seq=b).reshape(b, t, d)
    assert t >= CONV_WIDTH - 1
    new_conv = u[:, t - (CONV_WIDTH - 1):, :]
    heads = lambda a: a.reshape(b, SB_HEADS, HEAD_DIM, t).transpose(0, 3, 1, 2)[None]
    return y, heads(kt), heads(vt), new_conv[None]


def kernel(x_prompt, x_sample, p_prompt, p_sample, cache_k, cache_v, state_conv, ffn1_norm, ffn1_w_gu, ffn1_w_down, mix_norm, w_in, conv_w, conv_b, conv_ln_g, conv_ln_b, w_out, ffn2_norm, ffn2_w_gu, ffn2_w_down, ple_norm, ple_gate_w, ple_w, final_norm):
    assert ffn1_norm.shape[0] == 1, "single-layer stack"
    row = lambda a: a.reshape(1, -1)
    w = dict(
        g1=ffn1_norm, wgu1=ffn1_w_gu[0].astype(BF16), wd1=ffn1_w_down[0].astype(BF16),
        gm=mix_norm, win=w_in[0][:, :2 * CONV_CH + SB_WIDTH].astype(BF16),
        wkvt=w_in[0][:, 2 * CONV_CH + SB_WIDTH:].T.astype(BF16),
        cw8=jnp.broadcast_to(conv_w[0][:, None, :], (CONV_WIDTH, SUBLANES, CONV_CH)),
        cb=conv_b, lg=conv_ln_g, lb=conv_ln_b, wo=w_out[0].astype(BF16),
        g2=ffn2_norm, wgu2=ffn2_w_gu[0].astype(BF16), wd2=ffn2_w_down[0].astype(BF16),
        gp=ple_norm, wpg=ple_gate_w[0].astype(BF16), wp=ple_w[0].astype(BF16), gf=row(final_norm))
    y_p, k_p, v_p, c_p = _layer(x_prompt, p_prompt[0], None, None, None, w,
                                tm=512, tm_conv=512, n_sub=8, n_pair=1, fc=256)
    y_s, k_s, v_s, c_s = _layer(x_sample, p_sample[0], cache_k[0], cache_v[0], state_conv[0], w,
                                tm=512, tm_conv=64, n_sub=1, n_pair=4, fc=256)
    return y_p, y_s, k_p, v_p, c_p, k_s, v_s, c_s
```
